```python
import math
import jax, jax.numpy as jnp
from jax import lax
import numpy as np

D_MODEL = 2048
BATCH = 4
SEQ = 2048
DEPTH = 4
DEC_BATCH = 32
DEC_SEQ = 1
PAST_LEN = 16384
PAGE_SIZE = 128

H_A = D_MODEL // 128
HD_A = 64
KV_A = H_A // 8
WINDOW = 128
ROPE_THETA = 10000.0
H_B = 4
DK_B = D_MODEL // 16
DV_B = D_MODEL // 16
H_C = 4
DK_C = D_MODEL // 32
DV_C = D_MODEL // 16
GLA_RANK = 16
GLA_NORMALIZER = 16.0
CHUNK = 64
W_A = H_A * HD_A
W_B = H_B * DV_B
W_C = H_C * DV_C
MIX = W_A + W_B + W_C
N_KEYS = 128
N_EXPERTS = N_KEYS * N_KEYS
PEER_HEADS = 8
PEER_TOPK = 16
D_QUERY = 256
D_HALF = D_QUERY // 2
PEER_BLOCK = 128
ALPHA = (2 * DEPTH) ** 0.25
BETA = (8 * DEPTH) ** -0.25
LN_EPS = 1e-5
RMS_EPS = 1e-6
MASK_VALUE = -1e30
F_FLOOR = 1e-30
VALUE_PARTS = ('v_a', 'i_b', 'v_c')

kernel_name = 'hymba_swa_hgrn2_gla_peer_step'


def _proj_layout():
    return [('q_a', W_A), ('k_a', KV_A * HD_A), ('v_a', KV_A * HD_A),
            ('q_b', H_B * DK_B), ('f_b', H_B * DK_B), ('i_b', W_B), ('g_b', W_B),
            ('q_c', H_C * DK_C), ('k_c', H_C * DK_C), ('v_c', W_C), ('g_c', W_C), ('a_c', GLA_RANK)]


def _split_points():
    sizes = [sz for _, sz in _proj_layout()]
    return [int(s) for s in np.cumsum(sizes)[:-1]]


def layer_norm(x, g, b):
    xf = x.astype(jnp.float32)
    mu = jnp.mean(xf, -1, keepdims=True)
    var = jnp.mean(jnp.square(xf - mu), -1, keepdims=True)
    return ((xf - mu) * lax.rsqrt(var + LN_EPS)).astype(x.dtype) * g + b


def rms_norm(x, w):
    xf = x.astype(jnp.float32)
    return (xf * lax.rsqrt(jnp.mean(xf * xf, -1, keepdims=True) + RMS_EPS)).astype(x.dtype) * w


def rope(x, pos):
    half = HD_A // 2
    inv = ROPE_THETA ** (-jnp.arange(half, dtype=jnp.float32) / half)
    ang = pos.astype(jnp.float32)[:, None] * inv[None, :]
    cos = jnp.cos(ang)[None, :, None, :]
    sin = jnp.sin(ang)[None, :, None, :]
    xf = x.astype(jnp.float32)
    x1, x2 = xf[..., :half], xf[..., half:]
    return jnp.concatenate([x1 * cos - x2 * sin, x2 * cos + x1 * sin], -1).astype(x.dtype)


def sink_softmax(s, valid, sink):
    s = jnp.where(valid, s, MASK_VALUE)
    m = jnp.maximum(jnp.max(s, -1, keepdims=True), sink)
    p = jnp.where(valid, jnp.exp(s - m), 0.0)
    return p / (jnp.sum(p, -1, keepdims=True) + jnp.exp(sink - m))


def window_attn_prompt(q, k, v, sinks):
    Bn, L = q.shape[:2]
    G = H_A // KV_A
    nb = L // WINDOW
    qb = q.reshape(Bn, nb, WINDOW, KV_A, G, HD_A)
    kb = k.reshape(Bn, nb, WINDOW, KV_A, HD_A)
    vb = v.reshape(Bn, nb, WINDOW, KV_A, HD_A)
    pad = ((0, 0), (1, 0), (0, 0), (0, 0), (0, 0))
    kc = jnp.concatenate([jnp.pad(kb, pad)[:, :-1], kb], axis=2)
    vc = jnp.concatenate([jnp.pad(vb, pad)[:, :-1], vb], axis=2)
    s = jnp.einsum('bnqkgd,bnskd->bnkgqs', qb, kc).astype(jnp.float32) * (HD_A ** -0.5)
    i = jnp.arange(WINDOW)[:, None]
    j = jnp.arange(2 * WINDOW)[None, :]
    diff = i + WINDOW - j
    band = (diff >= 0) & (diff <= WINDOW)
    blk = jnp.arange(nb)[:, None, None]
    valid = band[None] & ((blk > 0) | (j[None] >= WINDOW))
    sink = sinks.astype(jnp.float32).reshape(KV_A, G)[None, None, :, :, None, None]
    p = sink_softmax(s, valid[None, :, None, None], sink)
    o = jnp.einsum('bnkgqs,bnskd->bnqkgd', p.astype(v.dtype), vc)
    return o.reshape(Bn, L, H_A * HD_A)


def window_attn_sample(q, k_new, v_new, k_buf, v_buf, sinks):
    Bn, L = q.shape[:2]
    G = H_A // KV_A
    Wb = k_buf.shape[1]
    kc = jnp.concatenate([k_buf.astype(k_new.dtype), k_new], axis=1)
    vc = jnp.concatenate([v_buf.astype(v_new.dtype), v_new], axis=1)
    qg = q.reshape(Bn, L, KV_A, G, HD_A)
    s = jnp.einsum('bqkgd,bskd->bkgqs', qg, kc).astype(jnp.float32) * (HD_A ** -0.5)
    diff = jnp.arange(L)[:, None] + Wb - jnp.arange(Wb + L)[None, :]
    valid = (diff >= 0) & (diff <= WINDOW)
    sink = sinks.astype(jnp.float32).reshape(KV_A, G)[None, :, :, None, None]
    p = sink_softmax(s, valid, sink)
    o = jnp.einsum('bkgqs,bskd->bqkgd', p.astype(vc.dtype), vc).reshape(Bn, L, H_A * HD_A)
    return o, kc[:, -Wb:], vc[:, -Wb:]


def gated_recurrence(q, k, v, log_a, s0):
    Bn, L, H, dk = q.shape
    dv = v.shape[-1]
    C = math.gcd(L, CHUNK)
    n = L // C

    def chunks(t):
        return t.astype(jnp.float32).reshape(Bn, n, C, H, t.shape[-1]).transpose(1, 0, 3, 2, 4)

    causal = jnp.tril(jnp.ones((C, C), dtype=bool))[:, :, None]

    def step(S, xs):
        qc, kc, vc, ac = xs
        b = jnp.cumsum(ac, axis=2)
        o = jnp.einsum('bhtd,bhde->bhte', qc * jnp.exp(b), S)
        rel = jnp.where(causal, b[:, :, :, None, :] - b[:, :, None, :, :], 0.0)
        decay = jnp.where(causal, jnp.exp(rel), 0.0)
        att = jnp.einsum('bhtd,bhsd,bhtsd->bhts', qc, kc, decay)
        o = o + jnp.einsum('bhts,bhse->bhte', att, vc)
        b_end = b[:, :, -1:, :]
        S = jnp.exp(b_end[:, :, 0, :, None]) * S + jnp.einsum('bhsd,bhse->bhde', kc * jnp.exp(b_end - b), vc)
        return S, o

    S, o = lax.scan(step, s0.astype(jnp.float32), (chunks(q), chunks(k), chunks(v), chunks(log_a)))
    o = o.transpose(1, 0, 3, 2, 4).reshape(Bn, L, H, dv)
    return o.astype(v.dtype), S.astype(s0.dtype)


def peer(x, wq, keys, u, v):
    Bn, L, D = x.shape
    T = Bn * L
    xt = x.reshape(T, D)
    q = (xt @ wq).reshape(T, PEER_HEADS, 2, D_HALF)
    sc = jnp.einsum('thpd,hpnd->thpn', q, keys).astype(jnp.float32)
    s_top, i_top = lax.top_k(sc, PEER_TOPK)
    cand = (s_top[:, :, 0, :, None] + s_top[:, :, 1, None, :]).reshape(T, PEER_HEADS, PEER_TOPK * PEER_TOPK)
    cand_idx = (i_top[:, :, 0, :, None] * N_KEYS + i_top[:, :, 1, None, :]).reshape(T, PEER_HEADS, PEER_TOPK * PEER_TOPK)
    best, sel = lax.top_k(cand, PEER_TOPK)
    idx = jnp.take_along_axis(cand_idx, sel, axis=-1)
    gate = jax.nn.softmax(best, axis=-1)
    tb = math.gcd(T, PEER_BLOCK)
    nb = T // tb

    def block(args):
        xb, ib, gb = args
        h = jnp.einsum('td,thkd->thk', xb, jnp.take(u, ib, axis=0))
        w = (gb * jax.nn.gelu(h.astype(jnp.float32), approximate=False)).astype(xb.dtype)
        return jnp.einsum('thk,thkd->td', w, jnp.take(v, ib, axis=0))

    out = lax.map(block, (xt.reshape(nb, tb, D),
                          idx.reshape(nb, tb, PEER_HEADS, PEER_TOPK),
                          gate.reshape(nb, tb, PEER_HEADS, PEER_TOPK)))
    return out.reshape(Bn, L, D)


def trunk_layer(x, pos, p, lb, kv_buf, s_b, s_c):
    Bn, L, _ = x.shape
    proj = x @ p['w_in']
    (q_a, k_a, v_a, q_b, f_b, i_b, g_b, q_c, k_c, v_c, g_c, a_c) = jnp.split(proj, _split_points(), axis=-1)
    q_a = rope(q_a.reshape(Bn, L, H_A, HD_A), pos)
    k_a = rope(k_a.reshape(Bn, L, KV_A, HD_A), pos)
    v_a = v_a.reshape(Bn, L, KV_A, HD_A)
    if kv_buf is None:
        o_a = window_attn_prompt(q_a, k_a, v_a, p['sinks'])
        k_keep, v_keep = k_a[:, -WINDOW:], v_a[:, -WINDOW:]
    else:
        o_a, k_keep, v_keep = window_attn_sample(q_a, k_a, v_a, kv_buf[0], kv_buf[1], p['sinks'])
    if s_b is None:
        s_b = jnp.zeros((Bn, H_B, DK_B, DV_B), x.dtype)
        s_c = jnp.zeros((Bn, H_C, DK_C, DV_C), x.dtype)
    zf = f_b.astype(jnp.float32).reshape(Bn, L, H_B, DK_B)
    lbh = lb.reshape(H_B, DK_B)
    f_gate = lbh + (1.0 - lbh) * jax.nn.sigmoid(zf)
    log_f = jnp.log(jnp.maximum(f_gate, F_FLOOR))
    key_b = 1.0 - f_gate
    q_bh = jax.nn.silu(q_b.reshape(Bn, L, H_B, DK_B))
    o_b, s_b_new = gated_recurrence(q_bh, key_b, i_b.reshape(Bn, L, H_B, DV_B), log_f, s_b)
    o_b = rms_norm(o_b, p['hgrn_norm_w']).reshape(Bn, L, W_B) * jax.nn.silu(g_b)
    log_a = jax.nn.log_sigmoid((a_c @ p['gla_wa2'] + p['gla_ba']).astype(jnp.float32)) / GLA_NORMALIZER
    q_ch = q_c.reshape(Bn, L, H_C, DK_C) * (DK_C ** -0.5)
    o_c, s_c_new = gated_recurrence(q_ch, k_c.reshape(Bn, L, H_C, DK_C), v_c.reshape(Bn, L, H_C, DV_C),
                                    log_a.reshape(Bn, L, H_C, DK_C), s_c)
    o_c = rms_norm(o_c, p['gla_norm_w']).reshape(Bn, L, W_C) * jax.nn.silu(g_c)
    mix = jnp.concatenate([o_a, o_b, o_c], axis=-1) @ p['w_out']
    x = layer_norm(ALPHA * x + mix, p['ln1_g'], p['ln1_b'])
    ff = peer(x, p['peer_wq'], p['peer_keys'], p['peer_u'], p['peer_v'])
    x = layer_norm(ALPHA * x + ff, p['ln2_g'], p['ln2_b'])
    return x, k_keep, v_keep, s_b_new, s_c_new


def setup_inputs(seed: int = 0) -> dict:
    key = jax.random.key(seed)
    ks = jax.random.split(key, 24)
    f32 = jnp.float32

    def nrm(k, shape, s):
        return jax.random.normal(k, shape, f32) * s

    layout = _proj_layout()
    in_cols = sum(sz for _, sz in layout)
    col_scale = np.concatenate([np.full((sz,), BETA if name in VALUE_PARTS else 1.0, np.float32)
                                for name, sz in layout])
    win_buf = min(WINDOW, PAST_LEN)
    return {
        'x_prompt': nrm(ks[0], (BATCH, SEQ, D_MODEL), 1.0),
        'x_sample': nrm(ks[1], (DEC_BATCH, DEC_SEQ, D_MODEL), 1.0),
        'cache_k': nrm(ks[2], (DEPTH, DEC_BATCH, win_buf, KV_A, HD_A), 1.0),
        'cache_v': nrm(ks[3], (DEPTH, DEC_BATCH, win_buf, KV_A, HD_A), 1.0),
        'state_hgrn': nrm(ks[4], (DEPTH, DEC_BATCH, H_B, DK_B, DV_B), 0.5),
        'state_gla': nrm(ks[5], (DEPTH, DEC_BATCH, H_C, DK_C, DV_C), 0.5),
        'w_in': nrm(ks[6], (DEPTH, D_MODEL, in_cols), D_MODEL ** -0.5) * jnp.asarray(col_scale),
        'w_out': nrm(ks[7], (DEPTH, MIX, D_MODEL), BETA * MIX ** -0.5),
        'attn_sinks': nrm(ks[8], (DEPTH, H_A), 0.5),
        'hgrn_norm_w': 1.0 + nrm(ks[9], (DEPTH, H_B, DV_B), 0.02),
        'lb_logits': nrm(ks[10], (DEPTH, H_B * DK_B), 0.1),
        'gla_wa2': nrm(ks[11], (DEPTH, GLA_RANK, H_C * DK_C), GLA_RANK ** -0.5),
        'gla_ba': nrm(ks[12], (DEPTH, H_C * DK_C), 0.02),
        'gla_norm_w': 1.0 + nrm(ks[13], (DEPTH, H_C, DV_C), 0.02),
        'ln1_g': 1.0 + nrm(ks[14], (DEPTH, D_MODEL), 0.02),
        'ln1_b': nrm(ks[15], (DEPTH, D_MODEL), 0.02),
        'ln2_g': 1.0 + nrm(ks[16], (DEPTH, D_MODEL), 0.02),
        'ln2_b': nrm(ks[17], (DEPTH, D_MODEL), 0.02),
        'peer_wq': nrm(ks[18], (DEPTH, D_MODEL, PEER_HEADS * D_QUERY), D_MODEL ** -0.5),
        'peer_keys': nrm(ks[19], (DEPTH, PEER_HEADS, 2, N_KEYS, D_HALF), D_HALF ** -0.5),
        'peer_u': nrm(ks[20], (DEPTH, N_EXPERTS, D_MODEL), D_MODEL ** -0.5),
        'peer_v': nrm(ks[21], (DEPTH, N_EXPERTS, D_MODEL), BETA * PEER_HEADS ** -0.5),
    }


def reference(x_prompt, x_sample, cache_k, cache_v, state_hgrn, state_gla, w_in, w_out, attn_sinks,
              hgrn_norm_w, lb_logits, gla_wa2, gla_ba, gla_norm_w, ln1_g, ln1_b, ln2_g, ln2_b,
              peer_wq, peer_keys, peer_u, peer_v):
    sm = jax.nn.softmax(lb_logits.astype(jnp.float32), axis=0)
    lower = jnp.cumsum(sm, axis=0) - sm[0:1]
    pos_p = jnp.arange(x_prompt.shape[1], dtype=jnp.int32)
    pos_s = PAST_LEN + jnp.arange(x_sample.shape[1], dtype=jnp.int32)
    hp, hs = x_prompt, x_sample
    kp_l, vp_l, bp_l, cp_l, ks_l, vs_l, bs_l, cs_l = [], [], [], [], [], [], [], []
    for l in range(DEPTH):
        p = {'w_in': w_in[l], 'w_out': w_out[l], 'sinks': attn_sinks[l], 'hgrn_norm_w': hgrn_norm_w[l],
             'gla_wa2': gla_wa2[l], 'gla_ba': gla_ba[l], 'gla_norm_w': gla_norm_w[l],
             'ln1_g': ln1_g[l], 'ln1_b': ln1_b[l], 'ln2_g': ln2_g[l], 'ln2_b': ln2_b[l],
             'peer_wq': peer_wq[l], 'peer_keys': peer_keys[l], 'peer_u': peer_u[l], 'peer_v': peer_v[l]}
        hp, kp, vp, bp, cp = trunk_layer(hp, pos_p, p, lower[l], None, None, None)
        hs, ks_, vs_, bs_, cs_ = trunk_layer(hs, pos_s, p, lower[l], (cache_k[l], cache_v[l]),
                                             state_hgrn[l], state_gla[l])
        kp_l.append(kp); vp_l.append(vp); bp_l.append(bp); cp_l.append(cp)
        ks_l.append(ks_); vs_l.append(vs_); bs_l.append(bs_); cs_l.append(cs_)
    new_k_prompt = jnp.stack(kp_l)
    new_v_prompt = jnp.stack(vp_l)
    new_hgrn_prompt = jnp.stack(bp_l)
    new_gla_prompt = jnp.stack(cp_l)
    new_k_sample = jnp.stack(ks_l)
    new_v_sample = jnp.stack(vs_l)
    new_hgrn_sample = jnp.stack(bs_l)
    new_gla_sample = jnp.stack(cs_l)
    return (hp, hs, new_k_prompt, new_v_prompt, new_hgrn_prompt, new_gla_prompt,
            new_k_sample, new_v_sample, new_hgrn_sample, new_gla_sample)
```

```python
import functools
import math

import jax
import jax.numpy as jnp
import numpy as np
from jax import lax
from jax.experimental import pallas as pl
from jax.experimental.pallas import tpu as pltpu

F32 = jnp.float32
BF16 = jnp.bfloat16

D_MODEL = 2048
DEPTH = 4
H_A, HD_A, KV_A = 16, 64, 2
WINDOW = 128
ROPE_THETA = 10000.0
H_B, DK_B, DV_B = 4, 128, 128
H_C, DK_C, DV_C = 4, 64, 128
GLA_RANK = 16
GLA_NORMALIZER = 16.0
W_A, W_B, W_C = H_A * HD_A, H_B * DV_B, H_C * DV_C
N_KEYS = 128
N_EXPERTS = N_KEYS * N_KEYS
PEER_HEADS = 8
PEER_TOPK = 16
ALPHA = (2 * DEPTH) ** 0.25
LN_EPS = 1e-5
RMS_EPS = 1e-6
MASK_VALUE = -1e30
F_FLOOR = 1e-30
NEG_INF = float("-inf")

C_QA, C_KA, C_VA = 0, 1024, 1152
C_QB, C_FB, C_IB, C_GB = 1280, 1792, 2304, 2816
C_QC, C_KC, C_VC, C_GC, C_AC = 3328, 3584, 3840, 4352, 4864
PROJ_COLS = 4880

LANES = 128
SUBLANES = 8
VMEM_LIMIT = 56 * 1024 * 1024
PROJ_PAD = 4992
CHUNK = 128
N_LEVELS = 7
TOK_TILE = 768
EXP_TILE = 512
I_PER_TILE = EXP_TILE // N_KEYS


def _cparams(sem):
    return pltpu.CompilerParams(dimension_semantics=sem, vmem_limit_bytes=VMEM_LIMIT)


def _mm_kernel(a_ref, w_ref, o_ref, wb_ref):
    @pl.when(pl.program_id(1) == 0)
    def _():
        wb_ref[...] = w_ref[...].astype(BF16)

    o_ref[...] = jnp.dot(a_ref[...], wb_ref[...], preferred_element_type=F32)


def _matmul(a, w, n_out, tm, tn):
    m, k = a.shape
    return pl.pallas_call(
        _mm_kernel,
        grid=(n_out // tn, m // tm),
        in_specs=[pl.BlockSpec((tm, k), lambda j, i: (i, 0)),
                  pl.BlockSpec((k, tn), lambda j, i: (0, j))],
        out_specs=pl.BlockSpec((tm, tn), lambda j, i: (i, j)),
        out_shape=jax.ShapeDtypeStruct((m, n_out), F32),
        scratch_shapes=[pltpu.VMEM((k, tn), BF16)],
        compiler_params=_cparams(("arbitrary", "arbitrary")),
        name="matmul",
    )(a, w)


def _ln_kernel(x_ref, y_ref, g_ref, b_ref, o_ref, ob_ref):
    z = ALPHA * x_ref[...] + y_ref[...]
    mu = jnp.mean(z, axis=-1, keepdims=True)
    zc = z - mu
    var = jnp.mean(zc * zc, axis=-1, keepdims=True)
    out = zc * lax.rsqrt(var + LN_EPS) * g_ref[...] + b_ref[...]
    o_ref[...] = out
    ob_ref[...] = out.astype(BF16)


def _res_layernorm(x, y, g, b, tm=256):
    m, d = x.shape
    row = pl.BlockSpec((tm, d), lambda i: (i, 0))
    vec = pl.BlockSpec((1, d), lambda i: (0, 0))
    return pl.pallas_call(
        _ln_kernel,
        grid=(m // tm,),
        in_specs=[row, row, vec, vec],
        out_specs=[row, row],
        out_shape=[jax.ShapeDtypeStruct((m, d), F32), jax.ShapeDtypeStruct((m, d), BF16)],
        compiler_params=_cparams(("arbitrary",)),
        name="res_layernorm",
    )(x, y, g.reshape(1, d), b.reshape(1, d))


def _rope_tables(pos):
    half = HD_A // 2
    inv = ROPE_THETA ** (-jnp.arange(half, dtype=F32) / half)
    ang = pos.astype(F32)[:, None] * inv[None, :]
    cos, sin = jnp.cos(ang), jnp.sin(ang)
    return jnp.tile(cos, (1, 4)), jnp.tile(jnp.concatenate([-sin, sin], axis=1), (1, 2))


def _rope128(x, cos, sin_signed):
    lane = lax.broadcasted_iota(jnp.int32, x.shape, 1)
    first = (lane % HD_A) < (HD_A // 2)
    partner = jnp.where(first, pltpu.roll(x, LANES - 32, 1), pltpu.roll(x, 32, 1))
    return x * cos + partner * sin_signed


def _dup_half(x, g):
    lane = lax.broadcasted_iota(jnp.int32, x.shape, 1)
    keep = (lane >= g * HD_A) & (lane < (g + 1) * HD_A)
    return jnp.where(keep, x, pltpu.roll(x, HD_A, 1))


def _attn_prompt_kernel(sink_ref, q_ref, kv_ref, cos_ref, sin_ref,
                        o_ref, kk_ref, vk_ref, kprev_ref, vprev_ref):
    n = pl.program_id(1)

    @pl.when(n == 0)
    def _():
        kprev_ref[...] = jnp.zeros_like(kprev_ref)
        vprev_ref[...] = jnp.zeros_like(vprev_ref)

    cos, sin = cos_ref[...], sin_ref[...]
    k_rot = _rope128(kv_ref[:, 0:LANES], cos, sin)
    v_cur = kv_ref[:, LANES:2 * LANES]
    k_prev, v_prev = kprev_ref[...], vprev_ref[...]

    qi = lax.broadcasted_iota(jnp.int32, (WINDOW, 2 * WINDOW), 0)
    kj = lax.broadcasted_iota(jnp.int32, (WINDOW, 2 * WINDOW), 1)
    first_key = jnp.where(n > 0, 0, WINDOW)
    valid = (kj >= jnp.maximum(qi, first_key)) & (kj <= qi + WINDOW)
    lane = lax.broadcasted_iota(jnp.int32, (WINDOW, LANES), 1)

    for g in range(KV_A):
        keys = jnp.concatenate([_dup_half(k_prev, g), _dup_half(k_rot, g)], axis=0).astype(BF16)
        vals = jnp.concatenate([_dup_half(v_prev, g), _dup_half(v_cur, g)], axis=0).astype(BF16)
        for pair in range(g * 4, g * 4 + 4):
            qp = _rope128(q_ref[:, pair * LANES:(pair + 1) * LANES], cos, sin) * (HD_A ** -0.5)
            outs = []
            for half in range(2):
                sink = sink_ref[2 * pair + half]
                qm = jnp.where((lane >= half * HD_A) & (lane < (half + 1) * HD_A), qp, 0.0).astype(BF16)
                s = lax.dot_general(qm, keys, (((1,), (1,)), ((), ())), preferred_element_type=F32)
                s = jnp.where(valid, s, MASK_VALUE)
                m = jnp.maximum(jnp.max(s, axis=-1, keepdims=True), sink)
                p = jnp.where(valid, jnp.exp(s - m), 0.0)
                denom = jnp.sum(p, axis=-1, keepdims=True) + jnp.exp(sink - m)
                o = jnp.dot(p.astype(BF16), vals, preferred_element_type=F32)
                outs.append(o / denom)
            o_ref[:, pair * LANES:(pair + 1) * LANES] = jnp.where(lane < HD_A, outs[0], outs[1]).astype(BF16)

    kprev_ref[...] = k_rot
    vprev_ref[...] = v_cur
    kk_ref[0] = k_rot
    vk_ref[0] = v_cur


def _attn_prompt(proj, sinks, cos, sin, n_batch, seq):
    nb = seq // WINDOW
    rows = n_batch * seq
    return pl.pallas_call(
        _attn_prompt_kernel,
        grid=(n_batch, nb),
        in_specs=[pl.BlockSpec(memory_space=pltpu.SMEM),
                  pl.BlockSpec((WINDOW, W_A), lambda b, n: (b * nb + n, 0)),
                  pl.BlockSpec((WINDOW, 2 * LANES), lambda b, n: (b * nb + n, C_KA // (2 * LANES))),
                  pl.BlockSpec((WINDOW, LANES), lambda b, n: (n, 0)),
                  pl.BlockSpec((WINDOW, LANES), lambda b, n: (n, 0))],
        out_specs=[pl.BlockSpec((WINDOW, W_A), lambda b, n: (b * nb + n, 0)),
                   pl.BlockSpec((1, WINDOW, LANES), lambda b, n: (b, 0, 0)),
                   pl.BlockSpec((1, WINDOW, LANES), lambda b, n: (b, 0, 0))],
        out_shape=[jax.ShapeDtypeStruct((rows, W_A), BF16),
                   jax.ShapeDtypeStruct((n_batch, WINDOW, LANES), F32),
                   jax.ShapeDtypeStruct((n_batch, WINDOW, LANES), F32)],
        scratch_shapes=[pltpu.VMEM((WINDOW, LANES), F32), pltpu.VMEM((WINDOW, LANES), F32)],
        compiler_params=_cparams(("arbitrary", "arbitrary")),
        name="attn_prompt",
    )(sinks, proj, proj, cos, sin)


def _level_constants():
    t = np.arange(CHUNK)
    w = np.zeros((N_LEVELS + 2, CHUNK, CHUNK), np.float32)
    msk = np.zeros((N_LEVELS + 1, CHUNK, CHUNK), np.float32)
    u = t[None, :]
    tt = t[:, None]
    for lv in range(N_LEVELS):
        m = 1 << lv
        anchor = (tt // (2 * m)) * 2 * m + m - 1
        w[lv] = np.where(tt > anchor, (u > anchor) & (u <= tt), (u > tt) & (u <= anchor))
        msk[lv] = ((tt // (2 * m)) == (u // (2 * m))) & ((tt % (2 * m)) >= m) & ((u % (2 * m)) < m)
    w[N_LEVELS] = u <= tt
    w[N_LEVELS + 1] = u > tt
    msk[N_LEVELS] = np.eye(CHUNK)
    return (jnp.asarray(w.reshape((N_LEVELS + 2) * CHUNK, CHUNK), BF16), jnp.asarray(msk, F32))


def _split3(x):
    hi = x.astype(BF16)
    r = x - hi.astype(F32)
    mid = r.astype(BF16)
    lo = (r - mid.astype(F32)).astype(BF16)
    return hi, mid, lo


def _chunk_core(q, k, la, heads, w_ref, msk_ref):
    hi, mid, lo = _split3(la)
    w = w_ref[...]
    dall = (jnp.dot(w, hi, preferred_element_type=F32) + jnp.dot(w, mid, preferred_element_type=F32)
            + jnp.dot(w, lo, preferred_element_type=F32))

    def blk(i):
        return dall[i * CHUNK:(i + 1) * CHUNK]

    qs = [(q * jnp.exp(blk(lv))) for lv in range(N_LEVELS)] + [q]
    ks = [(k * jnp.exp(blk(lv))).astype(BF16) for lv in range(N_LEVELS)] + [k.astype(BF16)]
    b = blk(N_LEVELS)
    q_in = q * jnp.exp(b)
    k_out = k * jnp.exp(blk(N_LEVELS + 1))
    decay_end = jnp.exp(b[CHUNK - 1:CHUNK, :])

    outs = []
    for lane_mask, v, st_ref in heads:
        def sel(x):
            return x if lane_mask is None else jnp.where(lane_mask, x, 0.0)

        att = jnp.zeros((CHUNK, CHUNK), F32)
        for lv in range(N_LEVELS + 1):
            r = lax.dot_general(sel(qs[lv]).astype(BF16), ks[lv], (((1,), (1,)), ((), ())),
                                preferred_element_type=F32)
            att = att + msk_ref[lv] * r
        vb = v.astype(BF16)
        st = st_ref[...]
        o = jnp.dot(att.astype(BF16), vb, preferred_element_type=F32)
        o = o + lax.dot_general(sel(q_in).astype(BF16), st.astype(BF16), (((1,), (1,)), ((), ())),
                                preferred_element_type=F32)
        upd = lax.dot_general(vb, sel(k_out).astype(BF16), (((0,), (0,)), ((), ())),
                              preferred_element_type=F32)
        st_ref[...] = st * decay_end + upd
        outs.append(o)
    return outs


def _sigmoid(x):
    return 1.0 / (1.0 + jnp.exp(-x))


def _silu(x):
    return x * _sigmoid(x)


def _log_sigmoid(x):
    return jnp.minimum(x, 0.0) - jnp.log1p(jnp.exp(-jnp.abs(x)))


def _rms_gate(o, w_row, gate):
    return o * lax.rsqrt(jnp.mean(o * o, axis=-1, keepdims=True) + RMS_EPS) * w_row * _silu(gate)


PAIR = 2 * LANES


def _hgrn_prompt_kernel(q_ref, f_ref, i_ref, g_ref, lb_ref, nw_ref, w_ref, msk_ref,
                        o_ref, s_ref, st_ref):
    c = pl.program_id(2)

    @pl.when(c == 0)
    def _():
        st_ref[...] = jnp.zeros_like(st_ref)

    for h in range(2):
        cols = slice(h * LANES, (h + 1) * LANES)
        lb = lb_ref[0, :, cols]
        f = lb + (1.0 - lb) * _sigmoid(f_ref[:, cols])
        la = jnp.log(jnp.maximum(f, F_FLOOR))
        (o,) = _chunk_core(_silu(q_ref[:, cols]), 1.0 - f, la, [(None, i_ref[:, cols], st_ref.at[h])],
                           w_ref, msk_ref)
        o_ref[:, cols] = _rms_gate(o, nw_ref[0, :, cols], g_ref[:, cols]).astype(BF16)

    @pl.when(c == pl.num_programs(2) - 1)
    def _():
        for h in range(2):
            s_ref[0, h] = st_ref[h].T


def _hgrn_prompt(proj, lb, norm_w, consts, n_batch, seq):
    nc = seq // CHUNK
    w, msk = consts

    def col(off):
        return pl.BlockSpec((CHUNK, PAIR), lambda b, p, c: (b * nc + c, off // PAIR + p))

    vec = pl.BlockSpec((1, 1, PAIR), lambda b, p, c: (p, 0, 0))
    return pl.pallas_call(
        _hgrn_prompt_kernel,
        grid=(n_batch, H_B // 2, nc),
        in_specs=[col(C_QB), col(C_FB), col(C_IB), col(C_GB), vec, vec,
                  pl.BlockSpec(w.shape, lambda b, p, c: (0, 0)),
                  pl.BlockSpec(msk.shape, lambda b, p, c: (0, 0, 0))],
        out_specs=[pl.BlockSpec((CHUNK, PAIR), lambda b, p, c: (b * nc + c, p)),
                   pl.BlockSpec((1, 2, DK_B, DV_B), lambda b, p, c: (b, p, 0, 0))],
        out_shape=[jax.ShapeDtypeStruct((n_batch * seq, W_B), BF16),
                   jax.ShapeDtypeStruct((n_batch, H_B, DK_B, DV_B), F32)],
        scratch_shapes=[pltpu.VMEM((2, DV_B, DK_B), F32)],
        compiler_params=_cparams(("arbitrary", "arbitrary", "arbitrary")),
        name="hgrn_prompt",
    )(proj, proj, proj, proj, lb.reshape(H_B // 2, 1, PAIR), norm_w.reshape(H_B // 2, 1, PAIR), w, msk)


def _gla_log_decay(a_blk, wa2, ba):
    lane = lax.broadcasted_iota(jnp.int32, a_blk.shape, 1)
    a = jnp.where(lane < GLA_RANK, a_blk, 0.0).astype(BF16)
    z = jnp.dot(a, wa2.astype(BF16), preferred_element_type=F32) + ba
    return _log_sigmoid(z) / GLA_NORMALIZER


def _gla_prompt_kernel(q_ref, k_ref, v_ref, g_ref, a_ref, wa2_ref, ba_ref, nw_ref, w_ref, msk_ref,
                       o_ref, s_ref, st_ref):
    c = pl.program_id(2)

    @pl.when(c == 0)
    def _():
        st_ref[...] = jnp.zeros_like(st_ref)

    la = _gla_log_decay(a_ref[...], wa2_ref[...], ba_ref[0])
    lane = lax.broadcasted_iota(jnp.int32, (CHUNK, LANES), 1)
    heads = []
    for half in range(2):
        mask = (lane >= half * DK_C) & (lane < (half + 1) * DK_C)
        heads.append((mask, v_ref[:, half * DV_C:(half + 1) * DV_C], st_ref.at[half]))
    outs = _chunk_core(q_ref[...] * (DK_C ** -0.5), k_ref[...], la, heads, w_ref, msk_ref)
    for half in range(2):
        hc = slice(half * DV_C, (half + 1) * DV_C)
        o_ref[:, hc] = _rms_gate(outs[half], nw_ref[0, :, hc], g_ref[:, hc]).astype(BF16)

    @pl.when(c == pl.num_programs(2) - 1)
    def _():
        for half in range(2):
            s_ref[0, half] = st_ref[half].T[half * DK_C:(half + 1) * DK_C, :]


def _gla_prompt(proj, wa2_pad, ba, norm_w, consts, n_batch, seq):
    nc = seq // CHUNK
    w, msk = consts
    n_pair = H_C // 2

    def col(width, off):
        return pl.BlockSpec((CHUNK, width), lambda b, p, c: (b * nc + c, off // width + p))

    def full(shape):
        return pl.BlockSpec(shape, lambda b, p, c: (0,) * len(shape))

    return pl.pallas_call(
        _gla_prompt_kernel,
        grid=(n_batch, n_pair, nc),
        in_specs=[col(LANES, C_QC), col(LANES, C_KC), col(PAIR, C_VC), col(PAIR, C_GC),
                  pl.BlockSpec((CHUNK, LANES), lambda b, p, c: (b * nc + c, C_AC // LANES)),
                  pl.BlockSpec((LANES, LANES), lambda b, p, c: (0, p)),
                  pl.BlockSpec((1, 1, LANES), lambda b, p, c: (p, 0, 0)),
                  pl.BlockSpec((1, 1, PAIR), lambda b, p, c: (p, 0, 0)),
                  full(w.shape), full(msk.shape)],
        out_specs=[pl.BlockSpec((CHUNK, PAIR), lambda b, p, c: (b * nc + c, p)),
                   pl.BlockSpec((1, 2, DK_C, DV_C), lambda b, p, c: (b, p, 0, 0))],
        out_shape=[jax.ShapeDtypeStruct((n_batch * seq, W_C), BF16),
                   jax.ShapeDtypeStruct((n_batch, H_C, DK_C, DV_C), F32)],
        scratch_shapes=[pltpu.VMEM((2, DV_C, LANES), F32)],
        compiler_params=_cparams(("arbitrary", "arbitrary", "arbitrary")),
        name="gla_prompt",
    )(proj, proj, proj, proj, proj, wa2_pad, ba.reshape(n_pair, 1, LANES), norm_w.reshape(n_pair, 1, PAIR),
      w, msk)


def _row_to_col(row):
    n = row.shape[1]
    r = lax.broadcasted_iota(jnp.int32, (n, n), 0)
    c = lax.broadcasted_iota(jnp.int32, (n, n), 1)
    return jnp.sum(jnp.where(r == c, jnp.broadcast_to(row, (n, n)), 0.0), axis=1, keepdims=True)


def _sample_kernel(p_ref, ck_ref, cv_ref, sb_ref, sc_ref, sink_ref, cos_ref, sin_ref, lb_ref, nwb_ref,
                   wa2_ref, ba_ref, nwc_ref,
                   mix_ref, ko_ref, vo_ref, sbo_ref, sco_ref):
    cos, sin = cos_ref[...], sin_ref[...]
    scale = HD_A ** -0.5

    k_new = _rope128(p_ref[0, :, C_KA:C_KA + LANES], cos, sin)
    v_new = p_ref[0, :, C_VA:C_VA + LANES]
    kc, vc = ck_ref[0], cv_ref[0]
    lane = lax.broadcasted_iota(jnp.int32, (1, LANES), 1)
    for pair in range(H_A // 2):
        g = pair // (H_A // 2 // KV_A)
        qp = _rope128(p_ref[0, :, pair * LANES:(pair + 1) * LANES], cos, sin) * scale
        kd, vd = _dup_half(kc, g).astype(BF16), _dup_half(vc, g).astype(BF16)
        knd, vnd = _dup_half(k_new, g), _dup_half(v_new, g)
        outs = []
        for half in range(2):
            sink = sink_ref[2 * pair + half]
            qm = jnp.where((lane >= half * HD_A) & (lane < (half + 1) * HD_A), qp, 0.0)
            qm8 = jnp.broadcast_to(qm, (SUBLANES, LANES)).astype(BF16)
            s_c = lax.dot_general(qm8, kd, (((1,), (1,)), ((), ())), preferred_element_type=F32)[0:1]
            s_n = jnp.sum(qm * knd, axis=-1, keepdims=True)
            m = jnp.maximum(jnp.maximum(jnp.max(s_c, axis=-1, keepdims=True), s_n), sink)
            p_c, p_n = jnp.exp(s_c - m), jnp.exp(s_n - m)
            denom = jnp.sum(p_c, axis=-1, keepdims=True) + p_n + jnp.exp(sink - m)
            pc8 = jnp.broadcast_to(p_c, (SUBLANES, WINDOW)).astype(BF16)
            o = jnp.dot(pc8, vd, preferred_element_type=F32)[0:1] + p_n * vnd
            outs.append(o / denom)
        mix_ref[0, :, pair * LANES:(pair + 1) * LANES] = jnp.where(lane < HD_A, outs[0], outs[1])
    ko_ref[0, 0:WINDOW - 1, :] = kc[1:WINDOW, :]
    ko_ref[0, WINDOW - 1:WINDOW, :] = k_new
    vo_ref[0, 0:WINDOW - 1, :] = vc[1:WINDOW, :]
    vo_ref[0, WINDOW - 1:WINDOW, :] = v_new

    for h in range(H_B):
        cols = slice(h * LANES, (h + 1) * LANES)
        lb = lb_ref[:, cols]
        f = lb + (1.0 - lb) * _sigmoid(p_ref[0, :, C_FB + h * LANES:C_FB + (h + 1) * LANES])
        a = jnp.exp(jnp.log(jnp.maximum(f, F_FLOOR)))
        q = _silu(p_ref[0, :, C_QB + h * LANES:C_QB + (h + 1) * LANES])
        v = p_ref[0, :, C_IB + h * LANES:C_IB + (h + 1) * LANES]
        s_new = _row_to_col(a) * sb_ref[0, h] + _row_to_col(1.0 - f) * v
        sbo_ref[0, h] = s_new
        o = jnp.sum(_row_to_col(q) * s_new, axis=0, keepdims=True)
        gate = p_ref[0, :, C_GB + h * LANES:C_GB + (h + 1) * LANES]
        mix_ref[0, :, W_A + h * LANES:W_A + (h + 1) * LANES] = _rms_gate(o, nwb_ref[:, cols], gate)

    la_all = _gla_log_decay(jnp.broadcast_to(p_ref[0, :, C_AC:C_AC + LANES], (SUBLANES, LANES)),
                            wa2_ref[...], ba_ref[...])[0:1]
    for pair in range(H_C // 2):
        cols = slice(pair * LANES, (pair + 1) * LANES)
        a_col = _row_to_col(jnp.exp(la_all[:, cols]))
        q_col = _row_to_col(p_ref[0, :, C_QC + pair * LANES:C_QC + (pair + 1) * LANES] * (DK_C ** -0.5))
        k_col = _row_to_col(p_ref[0, :, C_KC + pair * LANES:C_KC + (pair + 1) * LANES])
        for half in range(2):
            h = 2 * pair + half
            rows = slice(half * DK_C, (half + 1) * DK_C)
            v = p_ref[0, :, C_VC + h * DV_C:C_VC + (h + 1) * DV_C]
            s_new = a_col[rows] * sc_ref[0, h] + k_col[rows] * v
            sco_ref[0, h] = s_new
            o = jnp.sum(q_col[rows] * s_new, axis=0, keepdims=True)
            gate = p_ref[0, :, C_GC + h * DV_C:C_GC + (h + 1) * DV_C]
            hc = slice(h * DV_C, (h + 1) * DV_C)
            mix_ref[0, :, W_A + W_B + h * DV_C:W_A + W_B + (h + 1) * DV_C] = _rms_gate(o, nwc_ref[:, hc], gate)


def _sample_mixers(proj_s, cache_k, cache_v, s_b, s_c, sinks, cos, sin, lb, nw_b, wa2_pad, ba, nw_c):
    nseq = proj_s.shape[0]
    wqk = H_C * DK_C

    def per_seq(shape):
        return pl.BlockSpec((1,) + shape, lambda b: (b,) + (0,) * len(shape))

    def full(shape):
        return pl.BlockSpec(shape, lambda b: (0,) * len(shape))

    return pl.pallas_call(
        _sample_kernel,
        grid=(nseq,),
        in_specs=[per_seq((1, PROJ_PAD)), per_seq((WINDOW, LANES)), per_seq((WINDOW, LANES)),
                  per_seq((H_B, DK_B, DV_B)), per_seq((H_C, DK_C, DV_C)),
                  pl.BlockSpec(memory_space=pltpu.SMEM),
                  full((1, LANES)), full((1, LANES)), full((1, W_B)), full((1, W_B)),
                  full((LANES, wqk)), full((1, wqk)), full((1, W_C))],
        out_specs=[per_seq((1, D_MODEL)), per_seq((WINDOW, LANES)), per_seq((WINDOW, LANES)),
                   per_seq((H_B, DK_B, DV_B)), per_seq((H_C, DK_C, DV_C))],
        out_shape=[jax.ShapeDtypeStruct((nseq, 1, D_MODEL), F32),
                   jax.ShapeDtypeStruct((nseq, WINDOW, LANES), F32),
                   jax.ShapeDtypeStruct((nseq, WINDOW, LANES), F32),
                   jax.ShapeDtypeStruct((nseq, H_B, DK_B, DV_B), F32),
                   jax.ShapeDtypeStruct((nseq, H_C, DK_C, DV_C), F32)],
        compiler_params=_cparams(("arbitrary",)),
        name="sample_mixers",
    )(proj_s, cache_k, cache_v, s_b, s_c, sinks, cos, sin, lb.reshape(1, W_B), nw_b.reshape(1, W_B),
      wa2_pad, ba.reshape(1, wqk), nw_c.reshape(1, W_C))


def _top16_desc(cur, dst_ref):
    for r in range(PEER_TOPK):
        m = jnp.max(cur, axis=0, keepdims=True)
        dst_ref[r:r + 1, :] = m
        cur = jnp.where(cur >= m, NEG_INF, cur)


def _route_kernel(q_ref, keys_ref, s0_ref, s1_ref, ea_ref, eb_ref, thr_ref, sc_ref, v0_ref, v1_ref):
    tt = q_ref.shape[0]
    for p in range(2):
        qh = q_ref[:, p * LANES:(p + 1) * LANES].astype(BF16)
        sc_ref[p] = lax.dot_general(keys_ref[0, p].astype(BF16), qh, (((1,), (1,)), ((), ())),
                                    preferred_element_type=F32)

    def chunk(c, carry):
        cols = pl.ds(pl.multiple_of(c * LANES, LANES), LANES)
        s0, s1 = sc_ref[0, :, cols], sc_ref[1, :, cols]
        _top16_desc(s0, v0_ref)
        _top16_desc(s1, v1_ref)
        v0, v1 = v0_ref[...], v1_ref[...]
        cands = jnp.concatenate(
            [v0[0:1] + v1] + [v0[a:a + 1] + v1[0:8] for a in range(1, 8)] + [v0[8:16] + v1[0:1]], axis=0)
        cur = cands
        for r in range(PEER_TOPK):
            thr = jnp.max(cur, axis=0, keepdims=True)
            cur = jnp.where(cur >= thr, NEG_INF, cur)
        top = v0[0:1] + v1[0:1]
        z = jnp.sum(jnp.where(cands >= thr, jnp.exp(cands - top), 0.0), axis=0, keepdims=True)
        s0_ref[0, :, cols] = s0
        s1_ref[0, :, cols] = s1
        ea_ref[0, :, cols] = jnp.exp(s0 - v0[0:1]) / z
        eb_ref[0, :, cols] = jnp.exp(s1 - v1[0:1])
        thr_ref[0, :, cols] = thr
        return carry

    lax.fori_loop(0, tt // LANES, chunk, 0)


def _peer_route(qp, keys, tt=TOK_TILE):
    t = qp.shape[0]
    big = pl.BlockSpec((1, N_KEYS, tt), lambda i, h: (h, 0, i))
    big_shape = jax.ShapeDtypeStruct((PEER_HEADS, N_KEYS, t), F32)
    return pl.pallas_call(
        _route_kernel,
        grid=(t // tt, PEER_HEADS),
        in_specs=[pl.BlockSpec((tt, 2 * LANES), lambda i, h: (i, h)),
                  pl.BlockSpec((1, 2, N_KEYS, LANES), lambda i, h: (h, 0, 0, 0))],
        out_specs=[big, big, big, big, pl.BlockSpec((1, 1, tt), lambda i, h: (h, 0, i))],
        out_shape=[big_shape, big_shape, big_shape, big_shape,
                   jax.ShapeDtypeStruct((PEER_HEADS, 1, t), F32)],
        scratch_shapes=[pltpu.VMEM((2, N_KEYS, tt), F32), pltpu.VMEM((PEER_TOPK, LANES), F32),
                        pltpu.VMEM((PEER_TOPK, LANES), F32)],
        compiler_params=_cparams(("arbitrary", "arbitrary")),
        name="peer_route",
    )(qp, keys)


def _gelu(x):
    return 0.5 * x * (1.0 + lax.erf(x * (2.0 ** -0.5)))


def _peer_dense_kernel(x_ref, u_ref, v_ref, s0_ref, s1_ref, ea_ref, eb_ref, thr_ref, o_ref, h_ref, w_ref):
    e = pl.program_id(1)
    tt = x_ref.shape[0]

    @pl.when(e == 0)
    def _():
        o_ref[...] = jnp.zeros_like(o_ref)

    h_ref[...] = lax.dot_general(u_ref[...].astype(BF16), x_ref[...], (((1,), (1,)), ((), ())),
                                 preferred_element_type=F32)

    def chunk(c, carry):
        cols = pl.ds(pl.multiple_of(c * LANES, LANES), LANES)
        for il in range(I_PER_TILE):
            g = jnp.zeros((N_KEYS, LANES), F32)
            for h in range(PEER_HEADS):
                s0r = s0_ref[h, 0, il:il + 1, cols]
                ear = ea_ref[h, 0, il:il + 1, cols]
                sel = (s1_ref[h, :, cols] + s0r) >= thr_ref[h, :, cols]
                g = g + jnp.where(sel, eb_ref[h, :, cols] * ear, 0.0)
            rows = slice(il * N_KEYS, (il + 1) * N_KEYS)
            w_ref[rows, cols] = (g * _gelu(h_ref[rows, cols])).astype(BF16)
        return carry

    lax.fori_loop(0, tt // LANES, chunk, 0)
    o_ref[...] += lax.dot_general(w_ref[...], v_ref[...].astype(BF16), (((0,), (0,)), ((), ())),
                                  preferred_element_type=F32)


def _peer_dense(xb, u, v, s0, s1, ea, eb, thr, tt=TOK_TILE):
    t, d = xb.shape
    n_exp = u.shape[0]
    once = pl.Buffered(1)
    big = pl.BlockSpec((PEER_HEADS, N_KEYS, tt), lambda i, e: (0, 0, i), pipeline_mode=once)
    rows = pl.BlockSpec((PEER_HEADS, 1, I_PER_TILE, tt), lambda i, e: (0, e, 0, i))
    by_tile = (PEER_HEADS, N_KEYS // I_PER_TILE, I_PER_TILE, t)
    s0, ea = s0.reshape(by_tile), ea.reshape(by_tile)
    return pl.pallas_call(
        _peer_dense_kernel,
        grid=(t // tt, n_exp // EXP_TILE),
        in_specs=[pl.BlockSpec((tt, d), lambda i, e: (i, 0), pipeline_mode=once),
                  pl.BlockSpec((EXP_TILE, d), lambda i, e: (e, 0)),
                  pl.BlockSpec((EXP_TILE, d), lambda i, e: (e, 0)),
                  rows, big, rows, big,
                  pl.BlockSpec((PEER_HEADS, 1, tt), lambda i, e: (0, 0, i), pipeline_mode=once)],
        out_specs=pl.BlockSpec((tt, d), lambda i, e: (i, 0)),
        out_shape=jax.ShapeDtypeStruct((t, d), F32),
        scratch_shapes=[pltpu.VMEM((EXP_TILE, tt), F32), pltpu.VMEM((EXP_TILE, tt), BF16)],
        compiler_params=_cparams(("arbitrary", "arbitrary")),
        name="peer_dense",
    )(xb, u, v, s0, s1, ea, eb, thr)


def kernel(x_prompt, x_sample, cache_k, cache_v, state_hgrn, state_gla, w_in, w_out, attn_sinks,
           hgrn_norm_w, lb_logits, gla_wa2, gla_ba, gla_norm_w, ln1_g, ln1_b, ln2_g, ln2_b,
           peer_wq, peer_keys, peer_u, peer_v):
    n_batch, seq, d = x_prompt.shape
    n_dec = x_sample.shape[0]
    t_prompt = n_batch * seq
    t_real = t_prompt + n_dec
    tok_pad = -(-t_real // TOK_TILE) * TOK_TILE
    past_len = 16384

    sm = jax.nn.softmax(lb_logits.astype(F32), axis=0)
    lower = jnp.cumsum(sm, axis=0) - sm[0:1]
    cos_p, sin_p = _rope_tables(jnp.arange(seq, dtype=jnp.int32))
    cos_s, sin_s = _rope_tables(past_len + jnp.arange(1, dtype=jnp.int32))
    consts = _level_constants()

    x = jnp.concatenate([x_prompt.reshape(t_prompt, d), x_sample.reshape(n_dec, d),
                         jnp.zeros((tok_pad - t_real, d), F32)], axis=0)
    xb = x.astype(BF16)
    pad_rows = jnp.zeros((tok_pad - t_real, d), BF16)

    outs = [[] for _ in range(8)]
    for l in range(DEPTH):
        wa2_pad = jnp.zeros((LANES, H_C * DK_C), F32).at[:GLA_RANK].set(gla_wa2[l])
        proj = _matmul(xb, w_in[l], PROJ_PAD, tm=768, tn=1664)
        o_a, k_keep, v_keep = _attn_prompt(proj, attn_sinks[l], cos_p, sin_p, n_batch, seq)
        o_b, s_b = _hgrn_prompt(proj, lower[l], hgrn_norm_w[l], consts, n_batch, seq)
        o_c, s_c = _gla_prompt(proj, wa2_pad, gla_ba[l], gla_norm_w[l], consts, n_batch, seq)
        proj_s = proj[t_prompt:t_real].reshape(n_dec, 1, PROJ_PAD)
        mix_s, k_s, v_s, sb_s, sc_s = _sample_mixers(
            proj_s, cache_k[l].reshape(n_dec, WINDOW, LANES), cache_v[l].reshape(n_dec, WINDOW, LANES),
            state_hgrn[l], state_gla[l], attn_sinks[l], cos_s, sin_s, lower[l], hgrn_norm_w[l],
            wa2_pad, gla_ba[l], gla_norm_w[l])
        mix = jnp.concatenate([jnp.concatenate([o_a, o_b, o_c], axis=1), mix_s.reshape(n_dec, d).astype(BF16), pad_rows],
                              axis=0)
        y = _matmul(mix, w_out[l], d, tm=768, tn=1024)
        x1, x1b = _res_layernorm(x, y, ln1_g[l], ln1_b[l])
        qp = _matmul(x1b, peer_wq[l], d, tm=768, tn=1024)
        s0, s1, ea, eb, thr = _peer_route(qp, peer_keys[l])
        ff = _peer_dense(x1b, peer_u[l], peer_v[l], s0, s1, ea, eb, thr)
        x, xb = _res_layernorm(x1, ff, ln2_g[l], ln2_b[l])
        for lst, val in zip(outs, (k_keep.reshape(n_batch, WINDOW, KV_A, HD_A),
                                   v_keep.reshape(n_batch, WINDOW, KV_A, HD_A), s_b, s_c,
                                   k_s.reshape(n_dec, WINDOW, KV_A, HD_A),
                                   v_s.reshape(n_dec, WINDOW, KV_A, HD_A), sb_s, sc_s)):
            lst.append(val)

    y_prompt = x[:t_prompt].reshape(n_batch, seq, d)
    y_sample = x[t_prompt:t_real].reshape(n_dec, 1, d)
    return (y_prompt, y_sample) + tuple(jnp.stack(o) for o in outs)
```

```python
import functools
import math

import jax
import jax.numpy as jnp
import numpy as np
from jax import lax
from jax.experimental import pallas as pl
from jax.experimental.pallas import tpu as pltpu

F32 = jnp.float32
BF16 = jnp.bfloat16

D_MODEL = 2048
DEPTH = 4
H_A, HD_A, KV_A = 16, 64, 2
WINDOW = 128
ROPE_THETA = 10000.0
H_B, DK_B, DV_B = 4, 128, 128
H_C, DK_C, DV_C = 4, 64, 128
GLA_RANK = 16
GLA_NORMALIZER = 16.0
W_A, W_B, W_C = H_A * HD_A, H_B * DV_B, H_C * DV_C
N_KEYS = 128
N_EXPERTS = N_KEYS * N_KEYS
PEER_HEADS = 8
PEER_TOPK = 16
ALPHA = (2 * DEPTH) ** 0.25
LN_EPS = 1e-5
RMS_EPS = 1e-6
MASK_VALUE = -1e30
F_FLOOR = 1e-30
NEG_INF = float("-inf")

C_QA, C_KA, C_VA = 0, 1024, 1152
C_QB, C_FB, C_IB, C_GB = 1280, 1792, 2304, 2816
C_QC, C_KC, C_VC, C_GC, C_AC = 3328, 3584, 3840, 4352, 4864
PROJ_COLS = 4880

LANES = 128
SUBLANES = 8
VMEM_LIMIT = 56 * 1024 * 1024
PROJ_PAD = 4992
CHUNK = 128
N_LEVELS = 7
TOK_TILE = 768
EXP_TILE = 512
I_PER_TILE = EXP_TILE // N_KEYS


def _cparams(sem):
    return pltpu.CompilerParams(dimension_semantics=sem, vmem_limit_bytes=VMEM_LIMIT)


def _mm_kernel(a_ref, w_ref, o_ref, wb_ref):
    @pl.when(pl.program_id(1) == 0)
    def _():
        wb_ref[...] = w_ref[...].astype(BF16)

    o_ref[...] = jnp.dot(a_ref[...], wb_ref[...], preferred_element_type=F32)


def _matmul(a, w, layer, n_out, tm, tn):
    m, k = a.shape
    return pl.pallas_call(
        _mm_kernel,
        grid=(n_out // tn, m // tm),
        in_specs=[pl.BlockSpec((tm, k), lambda j, i: (i, 0)),
                  pl.BlockSpec((None, k, tn), lambda j, i: (layer, 0, j))],
        out_specs=pl.BlockSpec((tm, tn), lambda j, i: (i, j)),
        out_shape=jax.ShapeDtypeStruct((m, n_out), F32),
        scratch_shapes=[pltpu.VMEM((k, tn), BF16)],
        compiler_params=_cparams(("arbitrary", "arbitrary")),
        name="matmul",
    )(a, w)


def _ln_kernel(x_ref, y_ref, g_ref, b_ref, o_ref, ob_ref):
    z = ALPHA * x_ref[...] + y_ref[...]
    mu = jnp.mean(z, axis=-1, keepdims=True)
    zc = z - mu
    var = jnp.mean(zc * zc, axis=-1, keepdims=True)
    out = zc * lax.rsqrt(var + LN_EPS) * g_ref[...] + b_ref[...]
    o_ref[...] = out
    ob_ref[...] = out.astype(BF16)


def _res_layernorm(x, y, g, b, tm=256):
    m, d = x.shape
    row = pl.BlockSpec((tm, d), lambda i: (i, 0))
    vec = pl.BlockSpec((1, d), lambda i: (0, 0))
    return pl.pallas_call(
        _ln_kernel,
        grid=(m // tm,),
        in_specs=[row, row, vec, vec],
        out_specs=[row, row],
        out_shape=[jax.ShapeDtypeStruct((m, d), F32), jax.ShapeDtypeStruct((m, d), BF16)],
        compiler_params=_cparams(("arbitrary",)),
        name="res_layernorm",
    )(x, y, g.reshape(1, d), b.reshape(1, d))


def _rope_tables(pos):
    half = HD_A // 2
    inv = ROPE_THETA ** (-jnp.arange(half, dtype=F32) / half)
    ang = pos.astype(F32)[:, None] * inv[None, :]
    cos, sin = jnp.cos(ang), jnp.sin(ang)
    return jnp.tile(cos, (1, 4)), jnp.tile(jnp.concatenate([-sin, sin], axis=1), (1, 2))


def _rope128(x, cos, sin_signed):
    lane = lax.broadcasted_iota(jnp.int32, x.shape, 1)
    first = (lane % HD_A) < (HD_A // 2)
    partner = jnp.where(first, pltpu.roll(x, LANES - 32, 1), pltpu.roll(x, 32, 1))
    return x * cos + partner * sin_signed


def _dup_half(x, g):
    lane = lax.broadcasted_iota(jnp.int32, x.shape, 1)
    keep = (lane >= g * HD_A) & (lane < (g + 1) * HD_A)
    return jnp.where(keep, x, pltpu.roll(x, HD_A, 1))


def _attn_prompt_kernel(sink_ref, q_ref, kv_ref, cos_ref, sin_ref,
                        o_ref, kk_ref, vk_ref, kprev_ref, vprev_ref):
    n = pl.program_id(1)

    @pl.when(n == 0)
    def _():
        kprev_ref[...] = jnp.zeros_like(kprev_ref)
        vprev_ref[...] = jnp.zeros_like(vprev_ref)

    cos, sin = cos_ref[...], sin_ref[...]
    k_rot = _rope128(kv_ref[:, 0:LANES], cos, sin)
    v_cur = kv_ref[:, LANES:2 * LANES]
    k_prev, v_prev = kprev_ref[...], vprev_ref[...]

    qi = lax.broadcasted_iota(jnp.int32, (WINDOW, 2 * WINDOW), 0)
    kj = lax.broadcasted_iota(jnp.int32, (WINDOW, 2 * WINDOW), 1)
    first_key = jnp.where(n > 0, 0, WINDOW)
    valid = (kj >= jnp.maximum(qi, first_key)) & (kj <= qi + WINDOW)
    lane = lax.broadcasted_iota(jnp.int32, (WINDOW, LANES), 1)

    for g in range(KV_A):
        keys = jnp.concatenate([_dup_half(k_prev, g), _dup_half(k_rot, g)], axis=0).astype(BF16)
        vals = jnp.concatenate([_dup_half(v_prev, g), _dup_half(v_cur, g)], axis=0).astype(BF16)
        for pair in range(g * 4, g * 4 + 4):
            qp = _rope128(q_ref[:, pair * LANES:(pair + 1) * LANES], cos, sin) * (HD_A ** -0.5)
            outs = []
            for half in range(2):
                sink = sink_ref[2 * pair + half]
                qm = jnp.where((lane >= half * HD_A) & (lane < (half + 1) * HD_A), qp, 0.0).astype(BF16)
                s = lax.dot_general(qm, keys, (((1,), (1,)), ((), ())), preferred_element_type=F32)
                s = jnp.where(valid, s, MASK_VALUE)
                m = jnp.maximum(jnp.max(s, axis=-1, keepdims=True), sink)
                p = jnp.where(valid, jnp.exp(s - m), 0.0)
                denom = jnp.sum(p, axis=-1, keepdims=True) + jnp.exp(sink - m)
                o = jnp.dot(p.astype(BF16), vals, preferred_element_type=F32)
                outs.append(o / denom)
            o_ref[:, pair * LANES:(pair + 1) * LANES] = jnp.where(lane < HD_A, outs[0], outs[1]).astype(BF16)

    kprev_ref[...] = k_rot
    vprev_ref[...] = v_cur
    kk_ref[0] = k_rot
    vk_ref[0] = v_cur


def _attn_prompt(proj, sinks, cos, sin, n_batch, seq):
    nb = seq // WINDOW
    rows = n_batch * seq
    return pl.pallas_call(
        _attn_prompt_kernel,
        grid=(n_batch, nb),
        in_specs=[pl.BlockSpec(memory_space=pltpu.SMEM),
                  pl.BlockSpec((WINDOW, W_A), lambda b, n: (b * nb + n, 0)),
                  pl.BlockSpec((WINDOW, 2 * LANES), lambda b, n: (b * nb + n, C_KA // (2 * LANES))),
                  pl.BlockSpec((WINDOW, LANES), lambda b, n: (n, 0)),
                  pl.BlockSpec((WINDOW, LANES), lambda b, n: (n, 0))],
        out_specs=[pl.BlockSpec((WINDOW, W_A), lambda b, n: (b * nb + n, 0)),
                   pl.BlockSpec((1, WINDOW, LANES), lambda b, n: (b, 0, 0)),
                   pl.BlockSpec((1, WINDOW, LANES), lambda b, n: (b, 0, 0))],
        out_shape=[jax.ShapeDtypeStruct((rows, W_A), BF16),
                   jax.ShapeDtypeStruct((n_batch, WINDOW, LANES), F32),
                   jax.ShapeDtypeStruct((n_batch, WINDOW, LANES), F32)],
        scratch_shapes=[pltpu.VMEM((WINDOW, LANES), F32), pltpu.VMEM((WINDOW, LANES), F32)],
        compiler_params=_cparams(("arbitrary", "arbitrary")),
        name="attn_prompt",
    )(sinks, proj, proj, cos, sin)


def _level_constants():
    t = np.arange(CHUNK)
    w = np.zeros((N_LEVELS + 2, CHUNK, CHUNK), np.float32)
    msk = np.zeros((N_LEVELS + 1, CHUNK, CHUNK), np.float32)
    u = t[None, :]
    tt = t[:, None]
    for lv in range(N_LEVELS):
        m = 1 << lv
        anchor = (tt // (2 * m)) * 2 * m + m - 1
        w[lv] = np.where(tt > anchor, (u > anchor) & (u <= tt), (u > tt) & (u <= anchor))
        msk[lv] = ((tt // (2 * m)) == (u // (2 * m))) & ((tt % (2 * m)) >= m) & ((u % (2 * m)) < m)
    w[N_LEVELS] = u <= tt
    w[N_LEVELS + 1] = u > tt
    msk[N_LEVELS] = np.eye(CHUNK)
    w = w.reshape((N_LEVELS + 2) * CHUNK, CHUNK)
    return (jnp.asarray(np.concatenate([w, w, w], axis=1), BF16), jnp.asarray(msk, F32))


def _decay_sums(la, w_ref):
    hi = la.astype(BF16)
    r = la - hi.astype(F32)
    mid = r.astype(BF16)
    lo = (r - mid.astype(F32)).astype(BF16)
    return jnp.dot(w_ref[...], jnp.concatenate([hi, mid, lo], axis=0), preferred_element_type=F32)


def _chunk_core(q, k, dall, heads, msk_ref):
    def blk(i):
        return dall[i * CHUNK:(i + 1) * CHUNK]

    qs = [(q * jnp.exp(blk(lv))) for lv in range(N_LEVELS)] + [q]
    ks = [(k * jnp.exp(blk(lv))).astype(BF16) for lv in range(N_LEVELS)] + [k.astype(BF16)]
    b = blk(N_LEVELS)
    q_in = q * jnp.exp(b)
    k_out = k * jnp.exp(blk(N_LEVELS + 1))
    decay_end = jnp.exp(b[CHUNK - 1:CHUNK, :])

    outs = []
    for lane_mask, v, st_ref in heads:
        def sel(x):
            return x if lane_mask is None else jnp.where(lane_mask, x, 0.0)

        att = jnp.zeros((CHUNK, CHUNK), F32)
        for lv in range(N_LEVELS + 1):
            r = lax.dot_general(sel(qs[lv]).astype(BF16), ks[lv], (((1,), (1,)), ((), ())),
                                preferred_element_type=F32)
            att = att + msk_ref[lv] * r
        vb = v.astype(BF16)
        st = st_ref[...]
        o = jnp.dot(att.astype(BF16), vb, preferred_element_type=F32)
        o = o + lax.dot_general(sel(q_in).astype(BF16), st.astype(BF16), (((1,), (1,)), ((), ())),
                                preferred_element_type=F32)
        upd = lax.dot_general(vb, sel(k_out).astype(BF16), (((0,), (0,)), ((), ())),
                              preferred_element_type=F32)
        st_ref[...] = st * decay_end + upd
        outs.append(o)
    return outs


def _sigmoid(x):
    return 1.0 / (1.0 + jnp.exp(-x))


def _silu(x):
    return x * _sigmoid(x)


def _log_sigmoid(x):
    return jnp.minimum(x, 0.0) - jnp.log1p(jnp.exp(-jnp.abs(x)))


def _rms_gate(o, w_row, gate):
    return o * lax.rsqrt(jnp.mean(o * o, axis=-1, keepdims=True) + RMS_EPS) * w_row * _silu(gate)


PAIR = 2 * LANES


def _gla_log_decay(a_blk, wa2, ba):
    lane = lax.broadcasted_iota(jnp.int32, a_blk.shape, 1)
    a = jnp.where(lane < GLA_RANK, a_blk, 0.0).astype(BF16)
    z = jnp.dot(a, wa2.astype(BF16), preferred_element_type=F32) + ba
    return _log_sigmoid(z) / GLA_NORMALIZER


def _rec_prompt_kernel(qb_ref, fb_ref, ib_ref, gb_ref, lb_ref, nwb_ref,
                       qc_ref, kc_ref, vc_ref, gc_ref, ac_ref, wa2_ref, ba_ref, nwc_ref, w_ref, msk_ref,
                       ob_ref, oc_ref, sb_ref, sc_ref, stb_ref, stc_ref):
    c = pl.program_id(2)

    @pl.when(c == 0)
    def _():
        stb_ref[...] = jnp.zeros_like(stb_ref)
        stc_ref[...] = jnp.zeros_like(stc_ref)

    lb = lb_ref[0]
    f = lb + (1.0 - lb) * _sigmoid(fb_ref[...])
    la_b = jnp.log(jnp.maximum(f, F_FLOOR))
    la_c = _gla_log_decay(ac_ref[...], wa2_ref[...], ba_ref[0])
    dall = _decay_sums(jnp.concatenate([la_b, la_c], axis=1), w_ref)

    for h in range(2):
        cols = slice(h * LANES, (h + 1) * LANES)
        (o,) = _chunk_core(_silu(qb_ref[:, cols]), 1.0 - f[:, cols], dall[:, cols],
                           [(None, ib_ref[:, cols], stb_ref.at[h])], msk_ref)
        ob_ref[:, cols] = _rms_gate(o, nwb_ref[0, :, cols], gb_ref[:, cols]).astype(BF16)

    lane = lax.broadcasted_iota(jnp.int32, (CHUNK, LANES), 1)
    heads = []
    for half in range(2):
        mask = (lane >= half * DK_C) & (lane < (half + 1) * DK_C)
        heads.append((mask, vc_ref[:, half * DV_C:(half + 1) * DV_C], stc_ref.at[half]))
    outs = _chunk_core(qc_ref[...] * (DK_C ** -0.5), kc_ref[...], dall[:, PAIR:PAIR + LANES], heads, msk_ref)
    for half in range(2):
        hc = slice(half * DV_C, (half + 1) * DV_C)
        oc_ref[:, hc] = _rms_gate(outs[half], nwc_ref[0, :, hc], gc_ref[:, hc]).astype(BF16)

    @pl.when(c == pl.num_programs(2) - 1)
    def _():
        for h in range(2):
            sb_ref[0, h] = stb_ref[h].T
            sc_ref[0, h] = stc_ref[h].T[h * DK_C:(h + 1) * DK_C, :]


def _rec_prompt(proj, lb, nw_b, wa2_pad, ba, nw_c, consts, n_batch, seq):
    nc = seq // CHUNK
    w, msk = consts
    n_pair = H_B // 2
    assert H_C // 2 == n_pair

    def col(width, off):
        return pl.BlockSpec((CHUNK, width), lambda b, p, c: (b * nc + c, off // width + p))

    def full(shape):
        return pl.BlockSpec(shape, lambda b, p, c: (0,) * len(shape))

    def vec(width):
        return pl.BlockSpec((1, 1, width), lambda b, p, c: (p, 0, 0))

    out_rows = pl.BlockSpec((CHUNK, PAIR), lambda b, p, c: (b * nc + c, p))
    return pl.pallas_call(
        _rec_prompt_kernel,
        grid=(n_batch, n_pair, nc),
        in_specs=[col(PAIR, C_QB), col(PAIR, C_FB), col(PAIR, C_IB), col(PAIR, C_GB), vec(PAIR), vec(PAIR),
                  col(LANES, C_QC), col(LANES, C_KC), col(PAIR, C_VC), col(PAIR, C_GC),
                  pl.BlockSpec((CHUNK, LANES), lambda b, p, c: (b * nc + c, C_AC // LANES)),
                  pl.BlockSpec((LANES, LANES), lambda b, p, c: (0, p)), vec(LANES), vec(PAIR),
                  full(w.shape), full(msk.shape)],
        out_specs=[out_rows, out_rows,
                   pl.BlockSpec((1, 2, DK_B, DV_B), lambda b, p, c: (b, p, 0, 0)),
                   pl.BlockSpec((1, 2, DK_C, DV_C), lambda b, p, c: (b, p, 0, 0))],
        out_shape=[jax.ShapeDtypeStruct((n_batch * seq, W_B), BF16),
                   jax.ShapeDtypeStruct((n_batch * seq, W_C), BF16),
                   jax.ShapeDtypeStruct((n_batch, H_B, DK_B, DV_B), F32),
                   jax.ShapeDtypeStruct((n_batch, H_C, DK_C, DV_C), F32)],
        scratch_shapes=[pltpu.VMEM((2, DV_B, DK_B), F32), pltpu.VMEM((2, DV_C, LANES), F32)],
        compiler_params=_cparams(("arbitrary", "arbitrary", "arbitrary")),
        name="rec_prompt",
    )(proj, proj, proj, proj, lb.reshape(n_pair, 1, PAIR), nw_b.reshape(n_pair, 1, PAIR),
      proj, proj, proj, proj, proj, wa2_pad, ba.reshape(n_pair, 1, LANES), nw_c.reshape(n_pair, 1, PAIR),
      w, msk)


def _row_to_col(row):
    n = row.shape[1]
    r = lax.broadcasted_iota(jnp.int32, (n, n), 0)
    c = lax.broadcasted_iota(jnp.int32, (n, n), 1)
    return jnp.sum(jnp.where(r == c, jnp.broadcast_to(row, (n, n)), 0.0), axis=1, keepdims=True)


def _sample_kernel(p_ref, ck_ref, cv_ref, sb_ref, sc_ref, sink_ref, cos_ref, sin_ref, lb_ref, nwb_ref,
                   wa2_ref, ba_ref, nwc_ref,
                   mix_ref, ko_ref, vo_ref, sbo_ref, sco_ref):
    cos, sin = cos_ref[...], sin_ref[...]
    scale = HD_A ** -0.5

    k_new = _rope128(p_ref[0, :, C_KA:C_KA + LANES], cos, sin)
    v_new = p_ref[0, :, C_VA:C_VA + LANES]
    kc, vc = ck_ref[0], cv_ref[0]
    lane = lax.broadcasted_iota(jnp.int32, (1, LANES), 1)
    for pair in range(H_A // 2):
        g = pair // (H_A // 2 // KV_A)
        qp = _rope128(p_ref[0, :, pair * LANES:(pair + 1) * LANES], cos, sin) * scale
        kd, vd = _dup_half(kc, g).astype(BF16), _dup_half(vc, g).astype(BF16)
        knd, vnd = _dup_half(k_new, g), _dup_half(v_new, g)
        outs = []
        for half in range(2):
            sink = sink_ref[2 * pair + half]
            qm = jnp.where((lane >= half * HD_A) & (lane < (half + 1) * HD_A), qp, 0.0)
            qm8 = jnp.broadcast_to(qm, (SUBLANES, LANES)).astype(BF16)
            s_c = lax.dot_general(qm8, kd, (((1,), (1,)), ((), ())), preferred_element_type=F32)[0:1]
            s_n = jnp.sum(qm * knd, axis=-1, keepdims=True)
            m = jnp.maximum(jnp.maximum(jnp.max(s_c, axis=-1, keepdims=True), s_n), sink)
            p_c, p_n = jnp.exp(s_c - m), jnp.exp(s_n - m)
            denom = jnp.sum(p_c, axis=-1, keepdims=True) + p_n + jnp.exp(sink - m)
            pc8 = jnp.broadcast_to(p_c, (SUBLANES, WINDOW)).astype(BF16)
            o = jnp.dot(pc8, vd, preferred_element_type=F32)[0:1] + p_n * vnd
            outs.append(o / denom)
        mix_ref[0, :, pair * LANES:(pair + 1) * LANES] = jnp.where(lane < HD_A, outs[0], outs[1])
    ko_ref[0, 0:WINDOW - 1, :] = kc[1:WINDOW, :]
    ko_ref[0, WINDOW - 1:WINDOW, :] = k_new
    vo_ref[0, 0:WINDOW - 1, :] = vc[1:WINDOW, :]
    vo_ref[0, WINDOW - 1:WINDOW, :] = v_new

    for h in range(H_B):
        cols = slice(h * LANES, (h + 1) * LANES)
        lb = lb_ref[:, cols]
        f = lb + (1.0 - lb) * _sigmoid(p_ref[0, :, C_FB + h * LANES:C_FB + (h + 1) * LANES])
        a = jnp.exp(jnp.log(jnp.maximum(f, F_FLOOR)))
        q = _silu(p_ref[0, :, C_QB + h * LANES:C_QB + (h + 1) * LANES])
        v = p_ref[0, :, C_IB + h * LANES:C_IB + (h + 1) * LANES]
        s_new = _row_to_col(a) * sb_ref[0, h] + _row_to_col(1.0 - f) * v
        sbo_ref[0, h] = s_new
        o = jnp.sum(_row_to_col(q) * s_new, axis=0, keepdims=True)
        gate = p_ref[0, :, C_GB + h * LANES:C_GB + (h + 1) * LANES]
        mix_ref[0, :, W_A + h * LANES:W_A + (h + 1) * LANES] = _rms_gate(o, nwb_ref[:, cols], gate)

    la_all = _gla_log_decay(jnp.broadcast_to(p_ref[0, :, C_AC:C_AC + LANES], (SUBLANES, LANES)),
                            wa2_ref[...], ba_ref[...])[0:1]
    for pair in range(H_C // 2):
        cols = slice(pair * LANES, (pair + 1) * LANES)
        a_col = _row_to_col(jnp.exp(la_all[:, cols]))
        q_col = _row_to_col(p_ref[0, :, C_QC + pair * LANES:C_QC + (pair + 1) * LANES] * (DK_C ** -0.5))
        k_col = _row_to_col(p_ref[0, :, C_KC + pair * LANES:C_KC + (pair + 1) * LANES])
        for half in range(2):
            h = 2 * pair + half
            rows = slice(half * DK_C, (half + 1) * DK_C)
            v = p_ref[0, :, C_VC + h * DV_C:C_VC + (h + 1) * DV_C]
            s_new = a_col[rows] * sc_ref[0, h] + k_col[rows] * v
            sco_ref[0, h] = s_new
            o = jnp.sum(q_col[rows] * s_new, axis=0, keepdims=True)
            gate = p_ref[0, :, C_GC + h * DV_C:C_GC + (h + 1) * DV_C]
            hc = slice(h * DV_C, (h + 1) * DV_C)
            mix_ref[0, :, W_A + W_B + h * DV_C:W_A + W_B + (h + 1) * DV_C] = _rms_gate(o, nwc_ref[:, hc], gate)


def _sample_mixers(proj_s, cache_k, cache_v, s_b, s_c, layer, sinks, cos, sin, lb, nw_b, wa2_pad, ba, nw_c):
    nseq = proj_s.shape[0]
    wqk = H_C * DK_C

    def per_seq(shape):
        return pl.BlockSpec((1,) + shape, lambda b: (b,) + (0,) * len(shape))

    def per_seq_of_layer(shape):
        return pl.BlockSpec((None, 1) + shape, lambda b: (layer, b) + (0,) * len(shape))

    def full(shape):
        return pl.BlockSpec(shape, lambda b: (0,) * len(shape))

    return pl.pallas_call(
        _sample_kernel,
        grid=(nseq,),
        in_specs=[per_seq((1, PROJ_PAD)), per_seq_of_layer((WINDOW, LANES)), per_seq_of_layer((WINDOW, LANES)),
                  per_seq_of_layer((H_B, DK_B, DV_B)), per_seq_of_layer((H_C, DK_C, DV_C)),
                  pl.BlockSpec(memory_space=pltpu.SMEM),
                  full((1, LANES)), full((1, LANES)), full((1, W_B)), full((1, W_B)),
                  full((LANES, wqk)), full((1, wqk)), full((1, W_C))],
        out_specs=[per_seq((1, D_MODEL)), per_seq((WINDOW, LANES)), per_seq((WINDOW, LANES)),
                   per_seq((H_B, DK_B, DV_B)), per_seq((H_C, DK_C, DV_C))],
        out_shape=[jax.ShapeDtypeStruct((nseq, 1, D_MODEL), F32),
                   jax.ShapeDtypeStruct((nseq, WINDOW, LANES), F32),
                   jax.ShapeDtypeStruct((nseq, WINDOW, LANES), F32),
                   jax.ShapeDtypeStruct((nseq, H_B, DK_B, DV_B), F32),
                   jax.ShapeDtypeStruct((nseq, H_C, DK_C, DV_C), F32)],
        compiler_params=_cparams(("arbitrary",)),
        name="sample_mixers",
    )(proj_s, cache_k, cache_v, s_b, s_c, sinks, cos, sin, lb.reshape(1, W_B), nw_b.reshape(1, W_B),
      wa2_pad, ba.reshape(1, wqk), nw_c.reshape(1, W_C))


N_TOP = PEER_TOPK + 1
TOP_ROWS = 24


def _top_desc(cur, dst_ref):
    dst_ref[...] = jnp.full(dst_ref.shape, NEG_INF, F32)
    for r in range(N_TOP):
        m = jnp.max(cur, axis=0, keepdims=True)
        dst_ref[r:r + 1, :] = m
        cur = jnp.where(cur >= m, NEG_INF, cur)


def _route_kernel(q_ref, keys_ref, s0_ref, s1_ref, ea_ref, eb_ref, thr_ref, sc_ref, v0_ref, v1_ref):
    tt = q_ref.shape[0]
    for p in range(2):
        qh = q_ref[:, p * LANES:(p + 1) * LANES].astype(BF16)
        sc_ref[p] = lax.dot_general(keys_ref[0, p].astype(BF16), qh, (((1,), (1,)), ((), ())),
                                    preferred_element_type=F32)

    def chunk(c, carry):
        cols = pl.ds(pl.multiple_of(c * LANES, LANES), LANES)
        s0, s1 = sc_ref[0, :, cols], sc_ref[1, :, cols]
        _top_desc(s0, v0_ref)
        _top_desc(s1, v1_ref)
        v0, v1 = v0_ref[...], v1_ref[...]
        cands = jnp.concatenate(
            [v0[0:1] + v1] + [v0[a:a + 1] + v1[0:8] for a in range(1, 8)] + [v0[8:TOP_ROWS] + v1[0:1]], axis=0)
        cur = cands
        for r in range(N_TOP):
            below = jnp.max(cur, axis=0, keepdims=True)
            cur = jnp.where(cur >= below, NEG_INF, cur)
            if r == PEER_TOPK - 1:
                kth = below
        cut = jnp.where(below == NEG_INF, kth, 0.5 * (kth + below))
        top = v0[0:1] + v1[0:1]
        z = jnp.sum(jnp.where(cands >= kth, jnp.exp(cands - top), 0.0), axis=0, keepdims=True)
        s0_ref[0, :, cols] = s0
        s1_ref[0, :, cols] = s1
        ea_ref[0, :, cols] = jnp.exp(s0 - v0[0:1]) / z
        eb_ref[0, :, cols] = jnp.exp(s1 - v1[0:1])
        thr_ref[0, :, cols] = cut
        return carry

    lax.fori_loop(0, tt // LANES, chunk, 0)


def _peer_route(qp, keys, layer, tt=TOK_TILE):
    t = qp.shape[0]
    big = pl.BlockSpec((1, N_KEYS, tt), lambda i, h: (h, 0, i))
    big_shape = jax.ShapeDtypeStruct((PEER_HEADS, N_KEYS, t), F32)
    return pl.pallas_call(
        _route_kernel,
        grid=(t // tt, PEER_HEADS),
        in_specs=[pl.BlockSpec((tt, 2 * LANES), lambda i, h: (i, h)),
                  pl.BlockSpec((None, 1, 2, N_KEYS, LANES), lambda i, h: (layer, h, 0, 0, 0))],
        out_specs=[big, big, big, big, pl.BlockSpec((1, 1, tt), lambda i, h: (h, 0, i))],
        out_shape=[big_shape, big_shape, big_shape, big_shape,
                   jax.ShapeDtypeStruct((PEER_HEADS, 1, t), F32)],
        scratch_shapes=[pltpu.VMEM((2, N_KEYS, tt), F32), pltpu.VMEM((TOP_ROWS, LANES), F32),
                        pltpu.VMEM((TOP_ROWS, LANES), F32)],
        compiler_params=_cparams(("arbitrary", "arbitrary")),
        name="peer_route",
    )(qp, keys)


def _gelu(x):
    return 0.5 * x * (1.0 + lax.erf(x * (2.0 ** -0.5)))


def _peer_dense_kernel(x_ref, u_ref, v_ref, s0_ref, s1_ref, ea_ref, eb_ref, cut_ref, o_ref):
    e = pl.program_id(1)
    tt = x_ref.shape[0]

    @pl.when(e == 0)
    def _():
        o_ref[...] = jnp.zeros_like(o_ref)

    def step(row0):
        hid = lax.dot_general(u_ref[...], x_ref[...], (((1,), (1,)), ((), ())), preferred_element_type=F32)
        blocks = []
        for il in range(I_PER_TILE):
            rows = slice(il * N_KEYS, (il + 1) * N_KEYS)
            parts = []
            for c in range(tt // LANES):
                cols = slice(c * LANES, (c + 1) * LANES)
                g = jnp.zeros((N_KEYS, LANES), F32)
                for h in range(PEER_HEADS):
                    need = cut_ref[h, :, cols] - s0_ref[h, row0 + il:row0 + il + 1, cols]
                    gate = eb_ref[h, :, cols] * ea_ref[h, row0 + il:row0 + il + 1, cols]
                    g = g + jnp.where(s1_ref[h, :, cols] >= need, gate, 0.0)
                parts.append((g * _gelu(hid[rows, cols])).astype(BF16))
            blocks.append(jnp.concatenate(parts, axis=1))
        w = jnp.concatenate(blocks, axis=0)
        o_ref[...] += lax.dot_general(w, v_ref[...], (((0,), (0,)), ((), ())), preferred_element_type=F32)

    @pl.when(e % 2 == 0)
    def _():
        step(0)

    @pl.when(e % 2 == 1)
    def _():
        step(I_PER_TILE)


def _peer_dense(xb, u, v, layer, s0, s1, ea, eb, thr, tt=TOK_TILE):
    t, d = xb.shape
    n_tiles = u.shape[1] // EXP_TILE
    once = pl.Buffered(1)
    big = pl.BlockSpec((PEER_HEADS, N_KEYS, tt), lambda i, e: (0, 0, i), pipeline_mode=once)
    rows = pl.BlockSpec((PEER_HEADS, 2 * I_PER_TILE, tt), lambda i, e: (0, e // 2, i))
    return pl.pallas_call(
        _peer_dense_kernel,
        grid=(t // tt, n_tiles),
        in_specs=[pl.BlockSpec((tt, d), lambda i, e: (i, 0), pipeline_mode=once),
                  pl.BlockSpec((None, EXP_TILE, d), lambda i, e: (layer, e, 0)),
                  pl.BlockSpec((None, EXP_TILE, d), lambda i, e: (layer, e, 0)),
                  rows, big, rows, big,
                  pl.BlockSpec((PEER_HEADS, 1, tt), lambda i, e: (0, 0, i), pipeline_mode=once)],
        out_specs=pl.BlockSpec((tt, d), lambda i, e: (i, 0)),
        out_shape=jax.ShapeDtypeStruct((t, d), F32),
        compiler_params=_cparams(("arbitrary", "arbitrary")),
        name="peer_dense",
    )(xb, u, v, s0, s1, ea, eb, thr)


def kernel(x_prompt, x_sample, cache_k, cache_v, state_hgrn, state_gla, w_in, w_out, attn_sinks,
           hgrn_norm_w, lb_logits, gla_wa2, gla_ba, gla_norm_w, ln1_g, ln1_b, ln2_g, ln2_b,
           peer_wq, peer_keys, peer_u, peer_v):
    n_batch, seq, d = x_prompt.shape
    n_dec = x_sample.shape[0]
    t_prompt = n_batch * seq
    t_real = t_prompt + n_dec
    tok_pad = -(-t_real // TOK_TILE) * TOK_TILE
    past_len = 16384

    sm = jax.nn.softmax(lb_logits.astype(F32), axis=0)
    lower = jnp.cumsum(sm, axis=0) - sm[0:1]
    cos_p, sin_p = _rope_tables(jnp.arange(seq, dtype=jnp.int32))
    cos_s, sin_s = _rope_tables(past_len + jnp.arange(1, dtype=jnp.int32))
    consts = _level_constants()
    u_bf, v_bf = peer_u.astype(BF16), peer_v.astype(BF16)
    cache_k2 = cache_k.reshape(DEPTH, n_dec, WINDOW, LANES)
    cache_v2 = cache_v.reshape(DEPTH, n_dec, WINDOW, LANES)

    x = jnp.concatenate([x_prompt.reshape(t_prompt, d), x_sample.reshape(n_dec, d),
                         jnp.zeros((tok_pad - t_real, d), F32)], axis=0)
    xb = x.astype(BF16)
    pad_rows = jnp.zeros((tok_pad - t_real, d), BF16)

    outs = [[] for _ in range(8)]
    for l in range(DEPTH):
        wa2_pad = jnp.zeros((LANES, H_C * DK_C), F32).at[:GLA_RANK].set(gla_wa2[l])
        proj = _matmul(xb, w_in, l, PROJ_PAD, tm=768, tn=1664)
        o_a, k_keep, v_keep = _attn_prompt(proj, attn_sinks[l], cos_p, sin_p, n_batch, seq)
        o_b, o_c, s_b, s_c = _rec_prompt(proj, lower[l], hgrn_norm_w[l], wa2_pad, gla_ba[l], gla_norm_w[l],
                                         consts, n_batch, seq)
        proj_s = proj[t_prompt:t_real].reshape(n_dec, 1, PROJ_PAD)
        mix_s, k_s, v_s, sb_s, sc_s = _sample_mixers(
            proj_s, cache_k2, cache_v2, state_hgrn, state_gla, l, attn_sinks[l], cos_s, sin_s, lower[l],
            hgrn_norm_w[l], wa2_pad, gla_ba[l], gla_norm_w[l])
        mix = jnp.concatenate([jnp.concatenate([o_a, o_b, o_c], axis=1), mix_s.reshape(n_dec, d).astype(BF16), pad_rows],
                              axis=0)
        y = _matmul(mix, w_out, l, d, tm=768, tn=1024)
        x1, x1b = _res_layernorm(x, y, ln1_g[l], ln1_b[l])
        qp = _matmul(x1b, peer_wq, l, d, tm=768, tn=1024)
        s0, s1, ea, eb, thr = _peer_route(qp, peer_keys, l)
        ff = _peer_dense(x1b, u_bf, v_bf, l, s0, s1, ea, eb, thr)
        x, xb = _res_layernorm(x1, ff, ln2_g[l], ln2_b[l])
        for lst, val in zip(outs, (k_keep.reshape(n_batch, WINDOW, KV_A, HD_A),
                                   v_keep.reshape(n_batch, WINDOW, KV_A, HD_A), s_b, s_c,
                                   k_s.reshape(n_dec, WINDOW, KV_A, HD_A),
                                   v_s.reshape(n_dec, WINDOW, KV_A, HD_A), sb_s, sc_s)):
            lst.append(val)

    y_prompt = x[:t_prompt].reshape(n_batch, seq, d)
    y_sample = x[t_prompt:t_real].reshape(n_dec, 1, d)
    return (y_prompt, y_sample) + tuple(jnp.stack(o) for o in outs)
```

```python
import functools
import math

import jax
import jax.numpy as jnp
import numpy as np
from jax import lax
from jax.experimental import pallas as pl
from jax.experimental.pallas import tpu as pltpu

F32 = jnp.float32
BF16 = jnp.bfloat16

D_MODEL = 2048
DEPTH = 4
H_A, HD_A, KV_A = 16, 64, 2
WINDOW = 128
ROPE_THETA = 10000.0
H_B, DK_B, DV_B = 4, 128, 128
H_C, DK_C, DV_C = 4, 64, 128
GLA_RANK = 16
GLA_NORMALIZER = 16.0
W_A, W_B, W_C = H_A * HD_A, H_B * DV_B, H_C * DV_C
N_KEYS = 128
N_EXPERTS = N_KEYS * N_KEYS
PEER_HEADS = 8
PEER_TOPK = 16
ALPHA = (2 * DEPTH) ** 0.25
LN_EPS = 1e-5
RMS_EPS = 1e-6
MASK_VALUE = -1e30
F_FLOOR = 1e-30
NEG_INF = float("-inf")

C_QA, C_KA, C_VA = 0, 1024, 1152
C_QB, C_FB, C_IB, C_GB = 1280, 1792, 2304, 2816
C_QC, C_KC, C_VC, C_GC, C_AC = 3328, 3584, 3840, 4352, 4864
PROJ_COLS = 4880

LANES = 128
SUBLANES = 8
VMEM_LIMIT = 56 * 1024 * 1024
PROJ_PAD = 4992
CHUNK = 128
N_LEVELS = 7
TOK_TILE = 768
EXP_TILE = 512
I_PER_TILE = EXP_TILE // N_KEYS
SEQ_PER_STEP = 4


def _cparams(sem):
    return pltpu.CompilerParams(dimension_semantics=sem, vmem_limit_bytes=VMEM_LIMIT)


def _mm_kernel(a_ref, w_ref, o_ref, wb_ref):
    @pl.when(pl.program_id(1) == 0)
    def _():
        wb_ref[...] = w_ref[...].astype(BF16)

    o_ref[...] = jnp.dot(a_ref[...], wb_ref[...], preferred_element_type=F32)


def _matmul(a, w, layer, n_out, tm, tn):
    m, k = a.shape
    return pl.pallas_call(
        _mm_kernel,
        grid=(n_out // tn, m // tm),
        in_specs=[pl.BlockSpec((tm, k), lambda j, i: (i, 0)),
                  pl.BlockSpec((None, k, tn), lambda j, i: (layer, 0, j))],
        out_specs=pl.BlockSpec((tm, tn), lambda j, i: (i, j)),
        out_shape=jax.ShapeDtypeStruct((m, n_out), F32),
        scratch_shapes=[pltpu.VMEM((k, tn), BF16)],
        compiler_params=_cparams(("arbitrary", "arbitrary")),
        name="matmul",
    )(a, w)


def _res_ln(x, y, g, b):
    z = ALPHA * x + y
    mu = jnp.mean(z, axis=-1, keepdims=True)
    zc = z - mu
    var = jnp.mean(zc * zc, axis=-1, keepdims=True)
    return zc * lax.rsqrt(var + LN_EPS) * g + b


def _ln_kernel(x_ref, y_ref, g_ref, b_ref, o_ref, ob_ref):
    out = _res_ln(x_ref[...], y_ref[...], g_ref[...], b_ref[...])
    o_ref[...] = out
    ob_ref[...] = out.astype(BF16)


def _mm_ln_kernel(a_ref, w_ref, x_ref, g_ref, b_ref, o_ref, ob_ref, wb_ref):
    @pl.when(pl.program_id(0) == 0)
    def _():
        wb_ref[...] = w_ref[...].astype(BF16)

    y = jnp.dot(a_ref[...], wb_ref[...], preferred_element_type=F32)
    out = _res_ln(x_ref[...], y, g_ref[...], b_ref[...])
    o_ref[...] = out
    ob_ref[...] = out.astype(BF16)


def _matmul_res_layernorm(a, w, layer, x, g, b, tm=256):
    m, k = a.shape
    d = x.shape[1]
    row = pl.BlockSpec((tm, d), lambda i: (i, 0))
    vec = pl.BlockSpec((1, d), lambda i: (0, 0))
    return pl.pallas_call(
        _mm_ln_kernel,
        grid=(m // tm,),
        in_specs=[pl.BlockSpec((tm, k), lambda i: (i, 0)),
                  pl.BlockSpec((None, k, d), lambda i: (layer, 0, 0), pipeline_mode=pl.Buffered(1)),
                  row, vec, vec],
        out_specs=[row, row],
        out_shape=[jax.ShapeDtypeStruct((m, d), F32), jax.ShapeDtypeStruct((m, d), BF16)],
        scratch_shapes=[pltpu.VMEM((k, d), BF16)],
        compiler_params=_cparams(("arbitrary",)),
        name="matmul_res_layernorm",
    )(a, w, x, g.reshape(1, d), b.reshape(1, d))


def _res_layernorm(x, y, g, b, tm=256):
    m, d = x.shape
    row = pl.BlockSpec((tm, d), lambda i: (i, 0))
    vec = pl.BlockSpec((1, d), lambda i: (0, 0))
    return pl.pallas_call(
        _ln_kernel,
        grid=(m // tm,),
        in_specs=[row, row, vec, vec],
        out_specs=[row, row],
        out_shape=[jax.ShapeDtypeStruct((m, d), F32), jax.ShapeDtypeStruct((m, d), BF16)],
        compiler_params=_cparams(("arbitrary",)),
        name="res_layernorm",
    )(x, y, g.reshape(1, d), b.reshape(1, d))


def _rope_tables(pos):
    half = HD_A // 2
    inv = ROPE_THETA ** (-jnp.arange(half, dtype=F32) / half)
    ang = pos.astype(F32)[:, None] * inv[None, :]
    cos, sin = jnp.cos(ang), jnp.sin(ang)
    return jnp.tile(cos, (1, 4)), jnp.tile(jnp.concatenate([-sin, sin], axis=1), (1, 2))


def _rope128(x, cos, sin_signed):
    lane = lax.broadcasted_iota(jnp.int32, x.shape, 1)
    first = (lane % HD_A) < (HD_A // 2)
    partner = jnp.where(first, pltpu.roll(x, LANES - 32, 1), pltpu.roll(x, 32, 1))
    return x * cos + partner * sin_signed


def _dup_half(x, g):
    lane = lax.broadcasted_iota(jnp.int32, x.shape, 1)
    keep = (lane >= g * HD_A) & (lane < (g + 1) * HD_A)
    return jnp.where(keep, x, pltpu.roll(x, HD_A, 1))


def _attn_prompt_kernel(sink_ref, q_ref, kv_ref, cos_ref, sin_ref,
                        o_ref, kk_ref, vk_ref, kprev_ref, vprev_ref):
    n = pl.program_id(1)

    @pl.when(n == 0)
    def _():
        kprev_ref[...] = jnp.zeros_like(kprev_ref)
        vprev_ref[...] = jnp.zeros_like(vprev_ref)

    cos, sin = cos_ref[...], sin_ref[...]
    k_rot = _rope128(kv_ref[:, 0:LANES], cos, sin)
    v_cur = kv_ref[:, LANES:2 * LANES]
    k_prev, v_prev = kprev_ref[...], vprev_ref[...]

    qi = lax.broadcasted_iota(jnp.int32, (WINDOW, 2 * WINDOW), 0)
    kj = lax.broadcasted_iota(jnp.int32, (WINDOW, 2 * WINDOW), 1)
    first_key = jnp.where(n > 0, 0, WINDOW)
    valid = (kj >= jnp.maximum(qi, first_key)) & (kj <= qi + WINDOW)
    lane = lax.broadcasted_iota(jnp.int32, (WINDOW, LANES), 1)

    for g in range(KV_A):
        keys = jnp.concatenate([_dup_half(k_prev, g), _dup_half(k_rot, g)], axis=0).astype(BF16)
        vals = jnp.concatenate([_dup_half(v_prev, g), _dup_half(v_cur, g)], axis=0).astype(BF16)
        for pair in range(g * 4, g * 4 + 4):
            qp = _rope128(q_ref[:, pair * LANES:(pair + 1) * LANES], cos, sin) * (HD_A ** -0.5)
            outs = []
            for half in range(2):
                sink = sink_ref[2 * pair + half]
                qm = jnp.where((lane >= half * HD_A) & (lane < (half + 1) * HD_A), qp, 0.0).astype(BF16)
                s = lax.dot_general(qm, keys, (((1,), (1,)), ((), ())), preferred_element_type=F32)
                s = jnp.where(valid, s, MASK_VALUE)
                m = jnp.maximum(jnp.max(s, axis=-1, keepdims=True), sink)
                p = jnp.where(valid, jnp.exp(s - m), 0.0)
                denom = jnp.sum(p, axis=-1, keepdims=True) + jnp.exp(sink - m)
                o = jnp.dot(p.astype(BF16), vals, preferred_element_type=F32)
                outs.append(o / denom)
            o_ref[:, pair * LANES:(pair + 1) * LANES] = jnp.where(lane < HD_A, outs[0], outs[1]).astype(BF16)

    kprev_ref[...] = k_rot
    vprev_ref[...] = v_cur
    kk_ref[0] = k_rot
    vk_ref[0] = v_cur


def _attn_prompt(proj, sinks, cos, sin, n_batch, seq):
    nb = seq // WINDOW
    rows = n_batch * seq
    return pl.pallas_call(
        _attn_prompt_kernel,
        grid=(n_batch, nb),
        in_specs=[pl.BlockSpec(memory_space=pltpu.SMEM),
                  pl.BlockSpec((WINDOW, W_A), lambda b, n: (b * nb + n, 0)),
                  pl.BlockSpec((WINDOW, 2 * LANES), lambda b, n: (b * nb + n, C_KA // (2 * LANES))),
                  pl.BlockSpec((WINDOW, LANES), lambda b, n: (n, 0)),
                  pl.BlockSpec((WINDOW, LANES), lambda b, n: (n, 0))],
        out_specs=[pl.BlockSpec((WINDOW, W_A), lambda b, n: (b * nb + n, 0)),
                   pl.BlockSpec((1, WINDOW, LANES), lambda b, n: (b, 0, 0)),
                   pl.BlockSpec((1, WINDOW, LANES), lambda b, n: (b, 0, 0))],
        out_shape=[jax.ShapeDtypeStruct((rows, W_A), BF16),
                   jax.ShapeDtypeStruct((n_batch, WINDOW, LANES), F32),
                   jax.ShapeDtypeStruct((n_batch, WINDOW, LANES), F32)],
        scratch_shapes=[pltpu.VMEM((WINDOW, LANES), F32), pltpu.VMEM((WINDOW, LANES), F32)],
        compiler_params=_cparams(("arbitrary", "arbitrary")),
        name="attn_prompt",
    )(sinks, proj, proj, cos, sin)


def _level_constants():
    t = np.arange(CHUNK)
    w = np.zeros((N_LEVELS + 2, CHUNK, CHUNK), np.float32)
    msk = np.zeros((N_LEVELS + 1, CHUNK, CHUNK), np.float32)
    u = t[None, :]
    tt = t[:, None]
    for lv in range(N_LEVELS):
        m = 1 << lv
        anchor = (tt // (2 * m)) * 2 * m + m - 1
        w[lv] = np.where(tt > anchor, (u > anchor) & (u <= tt), (u > tt) & (u <= anchor))
        msk[lv] = ((tt // (2 * m)) == (u // (2 * m))) & ((tt % (2 * m)) >= m) & ((u % (2 * m)) < m)
    w[N_LEVELS] = u <= tt
    w[N_LEVELS + 1] = u > tt
    msk[N_LEVELS] = np.eye(CHUNK)
    w = w.reshape((N_LEVELS + 2) * CHUNK, CHUNK)
    return (jnp.asarray(np.concatenate([w, w, w], axis=1), BF16), jnp.asarray(msk, F32))


def _decay_sums(la, w_ref):
    hi = la.astype(BF16)
    r = la - hi.astype(F32)
    mid = r.astype(BF16)
    lo = (r - mid.astype(F32)).astype(BF16)
    return jnp.dot(w_ref[...], jnp.concatenate([hi, mid, lo], axis=0), preferred_element_type=F32)


def _chunk_core(q, k, dall, heads, msk_ref):
    def blk(i):
        return dall[i * CHUNK:(i + 1) * CHUNK]

    qs = [(q * jnp.exp(blk(lv))) for lv in range(N_LEVELS)] + [q]
    ks = [(k * jnp.exp(blk(lv))).astype(BF16) for lv in range(N_LEVELS)] + [k.astype(BF16)]
    b = blk(N_LEVELS)
    q_in = q * jnp.exp(b)
    k_out = k * jnp.exp(blk(N_LEVELS + 1))
    decay_end = jnp.exp(b[CHUNK - 1:CHUNK, :])

    outs = []
    for lane_mask, v, st_ref in heads:
        def sel(x):
            return x if lane_mask is None else jnp.where(lane_mask, x, 0.0)

        att = jnp.zeros((CHUNK, CHUNK), F32)
        for lv in range(N_LEVELS + 1):
            r = lax.dot_general(sel(qs[lv]).astype(BF16), ks[lv], (((1,), (1,)), ((), ())),
                                preferred_element_type=F32)
            att = att + msk_ref[lv] * r
        vb = v.astype(BF16)
        st = st_ref[...]
        o = jnp.dot(att.astype(BF16), vb, preferred_element_type=F32)
        o = o + lax.dot_general(sel(q_in).astype(BF16), st.astype(BF16), (((1,), (1,)), ((), ())),
                                preferred_element_type=F32)
        upd = lax.dot_general(vb, sel(k_out).astype(BF16), (((0,), (0,)), ((), ())),
                              preferred_element_type=F32)
        st_ref[...] = st * decay_end + upd
        outs.append(o)
    return outs


def _sigmoid(x):
    return 1.0 / (1.0 + jnp.exp(-x))


def _silu(x):
    return x * _sigmoid(x)


def _log_sigmoid(x):
    return jnp.minimum(x, 0.0) - jnp.log1p(jnp.exp(-jnp.abs(x)))


def _rms_gate(o, w_row, gate):
    return o * lax.rsqrt(jnp.mean(o * o, axis=-1, keepdims=True) + RMS_EPS) * w_row * _silu(gate)


PAIR = 2 * LANES


def _gla_log_decay(a_blk, wa2, ba):
    lane = lax.broadcasted_iota(jnp.int32, a_blk.shape, 1)
    a = jnp.where(lane < GLA_RANK, a_blk, 0.0).astype(BF16)
    z = jnp.dot(a, wa2.astype(BF16), preferred_element_type=F32) + ba
    return _log_sigmoid(z) / GLA_NORMALIZER


def _rec_prompt_kernel(qb_ref, fb_ref, ib_ref, gb_ref, lb_ref, nwb_ref,
                       qc_ref, kc_ref, vc_ref, gc_ref, ac_ref, wa2_ref, ba_ref, nwc_ref, w_ref, msk_ref,
                       ob_ref, oc_ref, sb_ref, sc_ref, stb_ref, stc_ref):
    c = pl.program_id(2)

    @pl.when(c == 0)
    def _():
        stb_ref[...] = jnp.zeros_like(stb_ref)
        stc_ref[...] = jnp.zeros_like(stc_ref)

    lb = lb_ref[0]
    f = lb + (1.0 - lb) * _sigmoid(fb_ref[...])
    la_b = jnp.log(jnp.maximum(f, F_FLOOR))
    la_c = _gla_log_decay(ac_ref[...], wa2_ref[...], ba_ref[0])
    dall = _decay_sums(jnp.concatenate([la_b, la_c], axis=1), w_ref)

    for h in range(2):
        cols = slice(h * LANES, (h + 1) * LANES)
        (o,) = _chunk_core(_silu(qb_ref[:, cols]), 1.0 - f[:, cols], dall[:, cols],
                           [(None, ib_ref[:, cols], stb_ref.at[h])], msk_ref)
        ob_ref[:, cols] = _rms_gate(o, nwb_ref[0, :, cols], gb_ref[:, cols]).astype(BF16)

    lane = lax.broadcasted_iota(jnp.int32, (CHUNK, LANES), 1)
    heads = []
    for half in range(2):
        mask = (lane >= half * DK_C) & (lane < (half + 1) * DK_C)
        heads.append((mask, vc_ref[:, half * DV_C:(half + 1) * DV_C], stc_ref.at[half]))
    outs = _chunk_core(qc_ref[...] * (DK_C ** -0.5), kc_ref[...], dall[:, PAIR:PAIR + LANES], heads, msk_ref)
    for half in range(2):
        hc = slice(half * DV_C, (half + 1) * DV_C)
        oc_ref[:, hc] = _rms_gate(outs[half], nwc_ref[0, :, hc], gc_ref[:, hc]).astype(BF16)

    @pl.when(c == pl.num_programs(2) - 1)
    def _():
        for h in range(2):
            sb_ref[0, h] = stb_ref[h].T
            sc_ref[0, h] = stc_ref[h].T[h * DK_C:(h + 1) * DK_C, :]


def _rec_prompt(proj, lb, nw_b, wa2_pad, ba, nw_c, consts, n_batch, seq):
    nc = seq // CHUNK
    w, msk = consts
    n_pair = H_B // 2
    assert H_C // 2 == n_pair

    def col(width, off):
        return pl.BlockSpec((CHUNK, width), lambda b, p, c: (b * nc + c, off // width + p))

    def full(shape):
        return pl.BlockSpec(shape, lambda b, p, c: (0,) * len(shape))

    def vec(width):
        return pl.BlockSpec((1, 1, width), lambda b, p, c: (p, 0, 0))

    out_rows = pl.BlockSpec((CHUNK, PAIR), lambda b, p, c: (b * nc + c, p))
    return pl.pallas_call(
        _rec_prompt_kernel,
        grid=(n_batch, n_pair, nc),
        in_specs=[col(PAIR, C_QB), col(PAIR, C_FB), col(PAIR, C_IB), col(PAIR, C_GB), vec(PAIR), vec(PAIR),
                  col(LANES, C_QC), col(LANES, C_KC), col(PAIR, C_VC), col(PAIR, C_GC),
                  pl.BlockSpec((CHUNK, LANES), lambda b, p, c: (b * nc + c, C_AC // LANES)),
                  pl.BlockSpec((LANES, LANES), lambda b, p, c: (0, p)), vec(LANES), vec(PAIR),
                  full(w.shape), full(msk.shape)],
        out_specs=[out_rows, out_rows,
                   pl.BlockSpec((1, 2, DK_B, DV_B), lambda b, p, c: (b, p, 0, 0)),
                   pl.BlockSpec((1, 2, DK_C, DV_C), lambda b, p, c: (b, p, 0, 0))],
        out_shape=[jax.ShapeDtypeStruct((n_batch * seq, W_B), BF16),
                   jax.ShapeDtypeStruct((n_batch * seq, W_C), BF16),
                   jax.ShapeDtypeStruct((n_batch, H_B, DK_B, DV_B), F32),
                   jax.ShapeDtypeStruct((n_batch, H_C, DK_C, DV_C), F32)],
        scratch_shapes=[pltpu.VMEM((2, DV_B, DK_B), F32), pltpu.VMEM((2, DV_C, LANES), F32)],
        compiler_params=_cparams(("arbitrary", "arbitrary", "arbitrary")),
        name="rec_prompt",
    )(proj, proj, proj, proj, lb.reshape(n_pair, 1, PAIR), nw_b.reshape(n_pair, 1, PAIR),
      proj, proj, proj, proj, proj, wa2_pad, ba.reshape(n_pair, 1, LANES), nw_c.reshape(n_pair, 1, PAIR),
      w, msk)


def _row_to_col(row):
    n = row.shape[1]
    r = lax.broadcasted_iota(jnp.int32, (n, n), 0)
    c = lax.broadcasted_iota(jnp.int32, (n, n), 1)
    return jnp.sum(jnp.where(r == c, jnp.broadcast_to(row, (n, n)), 0.0), axis=1, keepdims=True)


def _sample_kernel(p_ref, ck_ref, cv_ref, sb_ref, sc_ref, sink_ref, cos_ref, sin_ref, lb_ref, nwb_ref,
                   wa2_ref, ba_ref, nwc_ref,
                   mix_ref, ko_ref, vo_ref, sbo_ref, sco_ref):
    for n in range(p_ref.shape[0]):
        _sample_one(n, p_ref, ck_ref, cv_ref, sb_ref, sc_ref, sink_ref, cos_ref, sin_ref, lb_ref, nwb_ref,
                    wa2_ref, ba_ref, nwc_ref, mix_ref, ko_ref, vo_ref, sbo_ref, sco_ref)


def _sample_one(n, p_ref, ck_ref, cv_ref, sb_ref, sc_ref, sink_ref, cos_ref, sin_ref, lb_ref, nwb_ref,
                wa2_ref, ba_ref, nwc_ref, mix_ref, ko_ref, vo_ref, sbo_ref, sco_ref):
    cos, sin = cos_ref[...], sin_ref[...]
    scale = HD_A ** -0.5

    k_new = _rope128(p_ref[n, :, C_KA:C_KA + LANES], cos, sin)
    v_new = p_ref[n, :, C_VA:C_VA + LANES]
    kc, vc = ck_ref[n], cv_ref[n]
    lane = lax.broadcasted_iota(jnp.int32, (1, LANES), 1)
    heads_per_kv = H_A // KV_A
    head_row = lax.broadcasted_iota(jnp.int32, (heads_per_kv, 1), 0)
    for g in range(KV_A):
        pairs = range(g * heads_per_kv // 2, (g + 1) * heads_per_kv // 2)
        q_rows = []
        sink = jnp.zeros((heads_per_kv, 1), F32)
        for pair in pairs:
            qp = _rope128(p_ref[n, :, pair * LANES:(pair + 1) * LANES], cos, sin) * scale
            for half in range(2):
                sink = jnp.where(head_row == len(q_rows), sink_ref[2 * pair + half], sink)
                q_rows.append(jnp.where((lane >= half * HD_A) & (lane < (half + 1) * HD_A), qp, 0.0))
        qm = jnp.concatenate(q_rows, axis=0)
        kd, vd = _dup_half(kc, g).astype(BF16), _dup_half(vc, g).astype(BF16)
        knd, vnd = _dup_half(k_new, g), _dup_half(v_new, g)
        s_c = lax.dot_general(qm.astype(BF16), kd, (((1,), (1,)), ((), ())), preferred_element_type=F32)
        s_n = jnp.sum(qm * knd, axis=-1, keepdims=True)
        m = jnp.maximum(jnp.maximum(jnp.max(s_c, axis=-1, keepdims=True), s_n), sink)
        p_c, p_n = jnp.exp(s_c - m), jnp.exp(s_n - m)
        denom = jnp.sum(p_c, axis=-1, keepdims=True) + p_n + jnp.exp(sink - m)
        o = (jnp.dot(p_c.astype(BF16), vd, preferred_element_type=F32) + p_n * vnd) / denom
        for j, pair in enumerate(pairs):
            mix_ref[n, :, pair * LANES:(pair + 1) * LANES] = jnp.where(
                lane < HD_A, o[2 * j:2 * j + 1], o[2 * j + 1:2 * j + 2])
    ko_ref[n, 0:WINDOW - 1, :] = kc[1:WINDOW, :]
    ko_ref[n, WINDOW - 1:WINDOW, :] = k_new
    vo_ref[n, 0:WINDOW - 1, :] = vc[1:WINDOW, :]
    vo_ref[n, WINDOW - 1:WINDOW, :] = v_new

    for h in range(H_B):
        cols = slice(h * LANES, (h + 1) * LANES)
        lb = lb_ref[:, cols]
        f = lb + (1.0 - lb) * _sigmoid(p_ref[n, :, C_FB + h * LANES:C_FB + (h + 1) * LANES])
        a = jnp.exp(jnp.log(jnp.maximum(f, F_FLOOR)))
        q = _silu(p_ref[n, :, C_QB + h * LANES:C_QB + (h + 1) * LANES])
        v = p_ref[n, :, C_IB + h * LANES:C_IB + (h + 1) * LANES]
        s_new = _row_to_col(a) * sb_ref[n, h] + _row_to_col(1.0 - f) * v
        sbo_ref[n, h] = s_new
        o = jnp.sum(_row_to_col(q) * s_new, axis=0, keepdims=True)
        gate = p_ref[n, :, C_GB + h * LANES:C_GB + (h + 1) * LANES]
        mix_ref[n, :, W_A + h * LANES:W_A + (h + 1) * LANES] = _rms_gate(o, nwb_ref[:, cols], gate)

    la_all = _gla_log_decay(jnp.broadcast_to(p_ref[n, :, C_AC:C_AC + LANES], (SUBLANES, LANES)),
                            wa2_ref[...], ba_ref[...])[0:1]
    for pair in range(H_C // 2):
        cols = slice(pair * LANES, (pair + 1) * LANES)
        a_col = _row_to_col(jnp.exp(la_all[:, cols]))
        q_col = _row_to_col(p_ref[n, :, C_QC + pair * LANES:C_QC + (pair + 1) * LANES] * (DK_C ** -0.5))
        k_col = _row_to_col(p_ref[n, :, C_KC + pair * LANES:C_KC + (pair + 1) * LANES])
        for half in range(2):
            h = 2 * pair + half
            rows = slice(half * DK_C, (half + 1) * DK_C)
            v = p_ref[n, :, C_VC + h * DV_C:C_VC + (h + 1) * DV_C]
            s_new = a_col[rows] * sc_ref[n, h] + k_col[rows] * v
            sco_ref[n, h] = s_new
            o = jnp.sum(q_col[rows] * s_new, axis=0, keepdims=True)
            gate = p_ref[n, :, C_GC + h * DV_C:C_GC + (h + 1) * DV_C]
            hc = slice(h * DV_C, (h + 1) * DV_C)
            mix_ref[n, :, W_A + W_B + h * DV_C:W_A + W_B + (h + 1) * DV_C] = _rms_gate(o, nwc_ref[:, hc], gate)


def _sample_mixers(proj_s, cache_k, cache_v, s_b, s_c, layer, sinks, cos, sin, lb, nw_b, wa2_pad, ba, nw_c):
    nseq = proj_s.shape[0]
    wqk = H_C * DK_C

    def per_seq(shape):
        return pl.BlockSpec((SEQ_PER_STEP,) + shape, lambda b: (b,) + (0,) * len(shape))

    def per_seq_of_layer(shape):
        return pl.BlockSpec((None, SEQ_PER_STEP) + shape, lambda b: (layer, b) + (0,) * len(shape))

    def full(shape):
        return pl.BlockSpec(shape, lambda b: (0,) * len(shape))

    return pl.pallas_call(
        _sample_kernel,
        grid=(nseq // SEQ_PER_STEP,),
        in_specs=[per_seq((1, PROJ_PAD)), per_seq_of_layer((WINDOW, LANES)), per_seq_of_layer((WINDOW, LANES)),
                  per_seq_of_layer((H_B, DK_B, DV_B)), per_seq_of_layer((H_C, DK_C, DV_C)),
                  pl.BlockSpec(memory_space=pltpu.SMEM),
                  full((1, LANES)), full((1, LANES)), full((1, W_B)), full((1, W_B)),
                  full((LANES, wqk)), full((1, wqk)), full((1, W_C))],
        out_specs=[per_seq((1, D_MODEL)), per_seq((WINDOW, LANES)), per_seq((WINDOW, LANES)),
                   per_seq((H_B, DK_B, DV_B)), per_seq((H_C, DK_C, DV_C))],
        out_shape=[jax.ShapeDtypeStruct((nseq, 1, D_MODEL), F32),
                   jax.ShapeDtypeStruct((nseq, WINDOW, LANES), F32),
                   jax.ShapeDtypeStruct((nseq, WINDOW, LANES), F32),
                   jax.ShapeDtypeStruct((nseq, H_B, DK_B, DV_B), F32),
                   jax.ShapeDtypeStruct((nseq, H_C, DK_C, DV_C), F32)],
        compiler_params=_cparams(("arbitrary",)),
        name="sample_mixers",
    )(proj_s, cache_k, cache_v, s_b, s_c, sinks, cos, sin, lb.reshape(1, W_B), nw_b.reshape(1, W_B),
      wa2_pad, ba.reshape(1, wqk), nw_c.reshape(1, W_C))


N_TOP = PEER_TOPK + 1
TOP_ROWS = 24


SORT_N = 16


def _top_values(x, count):
    groups = [x[r:r + SUBLANES] for r in range(0, x.shape[0], SUBLANES)]
    groups += [jnp.full((SUBLANES, x.shape[1]), NEG_INF, F32)] * (SORT_N - len(groups))
    k = 2
    while k <= SORT_N:
        j = k // 2
        while j >= 1:
            for i in range(SORT_N):
                l = i ^ j
                if l > i:
                    hi, lo = jnp.maximum(groups[i], groups[l]), jnp.minimum(groups[i], groups[l])
                    groups[i], groups[l] = (hi, lo) if (i & k) == 0 else (lo, hi)
            j //= 2
        k *= 2
    out = []
    for r in range(count):
        m = jnp.max(groups[0], axis=0, keepdims=True)
        out.append(m)
        hit = groups[0] >= m
        depth = min(SORT_N, count - r)
        for d in range(depth - 1):
            groups[d] = jnp.where(hit, groups[d + 1], groups[d])
        groups[depth - 1] = jnp.where(hit, NEG_INF, groups[depth - 1])
    return out


def _top_desc(cur, dst_ref):
    dst_ref[...] = jnp.full(dst_ref.shape, NEG_INF, F32)
    for r, m in enumerate(_top_values(cur, N_TOP)):
        dst_ref[r:r + 1, :] = m


def _route_kernel(q_ref, keys_ref, s0_ref, s1_ref, ea_ref, eb_ref, thr_ref, sc_ref, v0_ref, v1_ref):
    tt = q_ref.shape[0]
    for p in range(2):
        qh = q_ref[:, p * LANES:(p + 1) * LANES].astype(BF16)
        sc_ref[p] = lax.dot_general(keys_ref[0, p].astype(BF16), qh, (((1,), (1,)), ((), ())),
                                    preferred_element_type=F32)

    def chunk(c, carry):
        cols = pl.ds(pl.multiple_of(c * LANES, LANES), LANES)
        s0, s1 = sc_ref[0, :, cols], sc_ref[1, :, cols]
        _top_desc(s0, v0_ref)
        _top_desc(s1, v1_ref)
        v0, v1 = v0_ref[...], v1_ref[...]
        cands = jnp.concatenate(
            [v0[0:1] + v1] + [v0[a:a + 1] + v1[0:8] for a in range(1, 8)] + [v0[8:TOP_ROWS] + v1[0:1]], axis=0)
        best = _top_values(cands, N_TOP)
        kth, below = best[PEER_TOPK - 1], best[PEER_TOPK]
        cut = jnp.where(below == NEG_INF, kth, 0.5 * (kth + below))
        top = v0[0:1] + v1[0:1]
        z = jnp.sum(jnp.where(cands >= kth, jnp.exp(cands - top), 0.0), axis=0, keepdims=True)
        s0_ref[0, :, cols] = s0
        s1_ref[0, :, cols] = s1
        ea_ref[0, :, cols] = jnp.exp(s0 - v0[0:1]) / z
        eb_ref[0, :, cols] = jnp.exp(s1 - v1[0:1])
        thr_ref[0, :, cols] = cut
        return carry

    lax.fori_loop(0, tt // LANES, chunk, 0)


def _peer_route(qp, keys, layer, tt=TOK_TILE):
    t = qp.shape[0]
    big = pl.BlockSpec((1, N_KEYS, tt), lambda i, h: (h, 0, i))
    big_shape = jax.ShapeDtypeStruct((PEER_HEADS, N_KEYS, t), F32)
    return pl.pallas_call(
        _route_kernel,
        grid=(t // tt, PEER_HEADS),
        in_specs=[pl.BlockSpec((tt, 2 * LANES), lambda i, h: (i, h)),
                  pl.BlockSpec((None, 1, 2, N_KEYS, LANES), lambda i, h: (layer, h, 0, 0, 0))],
        out_specs=[big, big, big, big, pl.BlockSpec((1, 1, tt), lambda i, h: (h, 0, i))],
        out_shape=[big_shape, big_shape, big_shape, big_shape,
                   jax.ShapeDtypeStruct((PEER_HEADS, 1, t), F32)],
        scratch_shapes=[pltpu.VMEM((2, N_KEYS, tt), F32), pltpu.VMEM((TOP_ROWS, LANES), F32),
                        pltpu.VMEM((TOP_ROWS, LANES), F32)],
        compiler_params=_cparams(("arbitrary", "arbitrary")),
        name="peer_route",
    )(qp, keys)


def _gelu(x):
    return 0.5 * x * (1.0 + lax.erf(x * (2.0 ** -0.5)))


def _peer_dense_kernel(x_ref, u_ref, v_ref, s0_ref, s1_ref, ea_ref, eb_ref, cut_ref, o_ref):
    e = pl.program_id(1)
    tt = x_ref.shape[0]

    @pl.when(e == 0)
    def _():
        o_ref[...] = jnp.zeros_like(o_ref)

    def step(row0):
        hid = lax.dot_general(u_ref[...], x_ref[...], (((1,), (1,)), ((), ())), preferred_element_type=F32)
        blocks = []
        for il in range(I_PER_TILE):
            rows = slice(il * N_KEYS, (il + 1) * N_KEYS)
            parts = []
            for c in range(tt // LANES):
                cols = slice(c * LANES, (c + 1) * LANES)
                g = jnp.zeros((N_KEYS, LANES), F32)
                for h in range(PEER_HEADS):
                    need = cut_ref[h, :, cols] - s0_ref[h, row0 + il:row0 + il + 1, cols]
                    gate = eb_ref[h, :, cols] * ea_ref[h, row0 + il:row0 + il + 1, cols]
                    g = g + jnp.where(s1_ref[h, :, cols] >= need, gate, 0.0)
                parts.append((g * _gelu(hid[rows, cols])).astype(BF16))
            blocks.append(jnp.concatenate(parts, axis=1))
        w = jnp.concatenate(blocks, axis=0)
        o_ref[...] += lax.dot_general(w, v_ref[...], (((0,), (0,)), ((), ())), preferred_element_type=F32)

    @pl.when(e % 2 == 0)
    def _():
        step(0)

    @pl.when(e % 2 == 1)
    def _():
        step(I_PER_TILE)


def _peer_dense(xb, u, v, layer, s0, s1, ea, eb, thr, tt=TOK_TILE):
    t, d = xb.shape
    n_tiles = u.shape[1] // EXP_TILE
    once = pl.Buffered(1)
    big = pl.BlockSpec((PEER_HEADS, N_KEYS, tt), lambda i, e: (0, 0, i), pipeline_mode=once)
    rows = pl.BlockSpec((PEER_HEADS, 2 * I_PER_TILE, tt), lambda i, e: (0, e // 2, i))
    return pl.pallas_call(
        _peer_dense_kernel,
        grid=(t // tt, n_tiles),
        in_specs=[pl.BlockSpec((tt, d), lambda i, e: (i, 0), pipeline_mode=once),
                  pl.BlockSpec((None, EXP_TILE, d), lambda i, e: (layer, e, 0)),
                  pl.BlockSpec((None, EXP_TILE, d), lambda i, e: (layer, e, 0)),
                  rows, big, rows, big,
                  pl.BlockSpec((PEER_HEADS, 1, tt), lambda i, e: (0, 0, i), pipeline_mode=once)],
        out_specs=pl.BlockSpec((tt, d), lambda i, e: (i, 0)),
        out_shape=jax.ShapeDtypeStruct((t, d), F32),
        compiler_params=_cparams(("arbitrary", "arbitrary")),
        name="peer_dense",
    )(xb, u, v, s0, s1, ea, eb, thr)


def kernel(x_prompt, x_sample, cache_k, cache_v, state_hgrn, state_gla, w_in, w_out, attn_sinks,
           hgrn_norm_w, lb_logits, gla_wa2, gla_ba, gla_norm_w, ln1_g, ln1_b, ln2_g, ln2_b,
           peer_wq, peer_keys, peer_u, peer_v):
    n_batch, seq, d = x_prompt.shape
    n_dec = x_sample.shape[0]
    t_prompt = n_batch * seq
    t_real = t_prompt + n_dec
    tok_pad = -(-t_real // TOK_TILE) * TOK_TILE
    past_len = 16384

    sm = jax.nn.softmax(lb_logits.astype(F32), axis=0)
    lower = jnp.cumsum(sm, axis=0) - sm[0:1]
    cos_p, sin_p = _rope_tables(jnp.arange(seq, dtype=jnp.int32))
    cos_s, sin_s = _rope_tables(past_len + jnp.arange(1, dtype=jnp.int32))
    consts = _level_constants()
    u_bf, v_bf = peer_u.astype(BF16), peer_v.astype(BF16)
    cache_k2 = cache_k.reshape(DEPTH, n_dec, WINDOW, LANES)
    cache_v2 = cache_v.reshape(DEPTH, n_dec, WINDOW, LANES)

    x = jnp.concatenate([x_prompt.reshape(t_prompt, d), x_sample.reshape(n_dec, d),
                         jnp.zeros((tok_pad - t_real, d), F32)], axis=0)
    xb = x.astype(BF16)
    pad_rows = jnp.zeros((tok_pad - t_real, d), BF16)

    outs = [[] for _ in range(8)]
    for l in range(DEPTH):
        wa2_pad = jnp.zeros((LANES, H_C * DK_C), F32).at[:GLA_RANK].set(gla_wa2[l])
        proj = _matmul(xb, w_in, l, PROJ_PAD, tm=768, tn=1664)
        o_a, k_keep, v_keep = _attn_prompt(proj, attn_sinks[l], cos_p, sin_p, n_batch, seq)
        o_b, o_c, s_b, s_c = _rec_prompt(proj, lower[l], hgrn_norm_w[l], wa2_pad, gla_ba[l], gla_norm_w[l],
                                         consts, n_batch, seq)
        proj_s = proj[t_prompt:t_real].reshape(n_dec, 1, PROJ_PAD)
        mix_s, k_s, v_s, sb_s, sc_s = _sample_mixers(
            proj_s, cache_k2, cache_v2, state_hgrn, state_gla, l, attn_sinks[l], cos_s, sin_s, lower[l],
            hgrn_norm_w[l], wa2_pad, gla_ba[l], gla_norm_w[l])
        mix = jnp.concatenate([jnp.concatenate([o_a, o_b, o_c], axis=1), mix_s.reshape(n_dec, d).astype(BF16), pad_rows],
                              axis=0)
        x1, x1b = _matmul_res_layernorm(mix, w_out, l, x, ln1_g[l], ln1_b[l])
        qp = _matmul(x1b, peer_wq, l, d, tm=768, tn=1024)
        s0, s1, ea, eb, thr = _peer_route(qp, peer_keys, l)
        ff = _peer_dense(x1b, u_bf, v_bf, l, s0, s1, ea, eb, thr)
        x, xb = _res_layernorm(x1, ff, ln2_g[l], ln2_b[l])
        for lst, val in zip(outs, (k_keep.reshape(n_batch, WINDOW, KV_A, HD_A),
                                   v_keep.reshape(n_batch, WINDOW, KV_A, HD_A), s_b, s_c,
                                   k_s.reshape(n_dec, WINDOW, KV_A, HD_A),
                                   v_s.reshape(n_dec, WINDOW, KV_A, HD_A), sb_s, sc_s)):
            lst.append(val)

    y_prompt = x[:t_prompt].reshape(n_batch, seq, d)
    y_sample = x[t_prompt:t_real].reshape(n_dec, 1, d)
    return (y_prompt, y_sample) + tuple(jnp.stack(o) for o in outs)
```

```python
import functools
import math

import jax
import jax.numpy as jnp
import numpy as np
from jax import lax
from jax.experimental import pallas as pl
from jax.experimental.pallas import tpu as pltpu

F32 = jnp.float32
BF16 = jnp.bfloat16

D_MODEL = 2048
DEPTH = 4
H_A, HD_A, KV_A = 16, 64, 2
WINDOW = 128
ROPE_THETA = 10000.0
H_B, DK_B, DV_B = 4, 128, 128
H_C, DK_C, DV_C = 4, 64, 128
GLA_RANK = 16
GLA_NORMALIZER = 16.0
W_A, W_B, W_C = H_A * HD_A, H_B * DV_B, H_C * DV_C
N_KEYS = 128
N_EXPERTS = N_KEYS * N_KEYS
PEER_HEADS = 8
PEER_TOPK = 16
ALPHA = (2 * DEPTH) ** 0.25
LN_EPS = 1e-5
RMS_EPS = 1e-6
MASK_VALUE = -1e30
F_FLOOR = 1e-30
NEG_INF = float("-inf")

C_QA, C_KA, C_VA = 0, 1024, 1152
C_QB, C_FB, C_IB, C_GB = 1280, 1792, 2304, 2816
C_QC, C_KC, C_VC, C_GC, C_AC = 3328, 3584, 3840, 4352, 4864
PROJ_COLS = 4880

LANES = 128
SUBLANES = 8
VMEM_LIMIT = 56 * 1024 * 1024
PROJ_PAD = 4992
CHUNK = 128
N_LEVELS = 7
TOK_TILE = 768
EXP_TILE = 512
I_PER_TILE = EXP_TILE // N_KEYS
SEQ_PER_STEP = 4


def _cparams(sem):
    return pltpu.CompilerParams(dimension_semantics=sem, vmem_limit_bytes=VMEM_LIMIT)


def _mm_kernel(a_ref, w_ref, o_ref, wb_ref):
    @pl.when(pl.program_id(1) == 0)
    def _():
        wb_ref[...] = w_ref[...].astype(BF16)

    o_ref[...] = jnp.dot(a_ref[...], wb_ref[...], preferred_element_type=F32)


def _matmul(a, w, layer, tm, tn):
    m, k = a.shape
    n = w.shape[2]
    return pl.pallas_call(
        _mm_kernel,
        grid=(n // tn, m // tm),
        in_specs=[pl.BlockSpec((tm, k), lambda j, i: (i, 0)),
                  pl.BlockSpec((None, k, tn), lambda j, i: (layer, 0, j))],
        out_specs=pl.BlockSpec((tm, tn), lambda j, i: (i, j)),
        out_shape=jax.ShapeDtypeStruct((m, n), F32),
        scratch_shapes=[pltpu.VMEM((k, tn), BF16)],
        compiler_params=_cparams(("arbitrary", "arbitrary")),
        name="matmul",
    )(a, w)


def _mm_nt_kernel(n_valid, a_ref, wt_ref, o_ref, wb_ref):
    j = pl.program_id(0)
    tn = o_ref.shape[1]

    @pl.when(pl.program_id(1) == 0)
    def _():
        wb_ref[...] = wt_ref[...].astype(BF16)

    out = lax.dot_general(a_ref[...], wb_ref[...], (((1,), (1,)), ((), ())), preferred_element_type=F32)
    col = j * tn + lax.broadcasted_iota(jnp.int32, out.shape, 1)
    o_ref[...] = jnp.where(col < n_valid, out, 0.0)


def _matmul_nt(a, wt, layer, n_out, tm, tn):
    m, k = a.shape
    return pl.pallas_call(
        functools.partial(_mm_nt_kernel, wt.shape[1]),
        grid=(n_out // tn, m // tm),
        in_specs=[pl.BlockSpec((tm, k), lambda j, i: (i, 0)),
                  pl.BlockSpec((None, tn, k), lambda j, i: (layer, j, 0))],
        out_specs=pl.BlockSpec((tm, tn), lambda j, i: (i, j)),
        out_shape=jax.ShapeDtypeStruct((m, n_out), F32),
        scratch_shapes=[pltpu.VMEM((tn, k), BF16)],
        compiler_params=_cparams(("arbitrary", "arbitrary")),
        name="matmul_nt",
    )(a, wt)


def _res_ln(x, y, g, b):
    z = ALPHA * x + y
    mu = jnp.mean(z, axis=-1, keepdims=True)
    zc = z - mu
    var = jnp.mean(zc * zc, axis=-1, keepdims=True)
    return zc * lax.rsqrt(var + LN_EPS) * g + b


def _mm_ln_kernel(a_ref, w_ref, x_ref, g_ref, b_ref, o_ref, ob_ref, wb_ref):
    @pl.when(pl.program_id(0) == 0)
    def _():
        wb_ref[...] = w_ref[...].astype(BF16)

    y = jnp.dot(a_ref[...], wb_ref[...], preferred_element_type=F32)
    out = _res_ln(x_ref[...], y, g_ref[...], b_ref[...])
    o_ref[...] = out
    ob_ref[...] = out.astype(BF16)


def _matmul_res_layernorm(a, w, layer, x, g, b, tm=256):
    m, k = a.shape
    d = x.shape[1]
    row = pl.BlockSpec((tm, d), lambda i: (i, 0))
    vec = pl.BlockSpec((1, d), lambda i: (0, 0))
    return pl.pallas_call(
        _mm_ln_kernel,
        grid=(m // tm,),
        in_specs=[pl.BlockSpec((tm, k), lambda i: (i, 0)),
                  pl.BlockSpec((None, k, d), lambda i: (layer, 0, 0), pipeline_mode=pl.Buffered(1)),
                  row, vec, vec],
        out_specs=[row, row],
        out_shape=[jax.ShapeDtypeStruct((m, d), F32), jax.ShapeDtypeStruct((m, d), BF16)],
        scratch_shapes=[pltpu.VMEM((k, d), BF16)],
        compiler_params=_cparams(("arbitrary",)),
        name="matmul_res_layernorm",
    )(a, w, x, g.reshape(1, d), b.reshape(1, d))


def _rope_tables(pos):
    half = HD_A // 2
    inv = ROPE_THETA ** (-jnp.arange(half, dtype=F32) / half)
    ang = pos.astype(F32)[:, None] * inv[None, :]
    cos, sin = jnp.cos(ang), jnp.sin(ang)
    return jnp.tile(cos, (1, 4)), jnp.tile(jnp.concatenate([-sin, sin], axis=1), (1, 2))


def _rope128(x, cos, sin_signed):
    lane = lax.broadcasted_iota(jnp.int32, x.shape, 1)
    first = (lane % HD_A) < (HD_A // 2)
    partner = jnp.where(first, pltpu.roll(x, LANES - 32, 1), pltpu.roll(x, 32, 1))
    return x * cos + partner * sin_signed


def _dup_half(x, g):
    lane = lax.broadcasted_iota(jnp.int32, x.shape, 1)
    keep = (lane >= g * HD_A) & (lane < (g + 1) * HD_A)
    return jnp.where(keep, x, pltpu.roll(x, HD_A, 1))


def _attn_prompt_kernel(sink_ref, q_ref, kv_ref, cos_ref, sin_ref,
                        o_ref, kk_ref, vk_ref, kprev_ref, vprev_ref):
    n = pl.program_id(1)

    @pl.when(n == 0)
    def _():
        kprev_ref[...] = jnp.zeros_like(kprev_ref)
        vprev_ref[...] = jnp.zeros_like(vprev_ref)

    cos, sin = cos_ref[...], sin_ref[...]
    k_rot = _rope128(kv_ref[:, 0:LANES], cos, sin)
    v_cur = kv_ref[:, LANES:2 * LANES]
    k_prev, v_prev = kprev_ref[...], vprev_ref[...]

    qi = lax.broadcasted_iota(jnp.int32, (WINDOW, 2 * WINDOW), 0)
    kj = lax.broadcasted_iota(jnp.int32, (WINDOW, 2 * WINDOW), 1)
    first_key = jnp.where(n > 0, 0, WINDOW)
    valid = (kj >= jnp.maximum(qi, first_key)) & (kj <= qi + WINDOW)
    lane = lax.broadcasted_iota(jnp.int32, (WINDOW, LANES), 1)

    for g in range(KV_A):
        keys = jnp.concatenate([_dup_half(k_prev, g), _dup_half(k_rot, g)], axis=0).astype(BF16)
        vals = jnp.concatenate([_dup_half(v_prev, g), _dup_half(v_cur, g)], axis=0).astype(BF16)
        for pair in range(g * 4, g * 4 + 4):
            qp = _rope128(q_ref[:, pair * LANES:(pair + 1) * LANES], cos, sin) * (HD_A ** -0.5)
            outs = []
            for half in range(2):
                sink = sink_ref[2 * pair + half]
                qm = jnp.where((lane >= half * HD_A) & (lane < (half + 1) * HD_A), qp, 0.0).astype(BF16)
                s = lax.dot_general(qm, keys, (((1,), (1,)), ((), ())), preferred_element_type=F32)
                s = jnp.where(valid, s, MASK_VALUE)
                m = jnp.maximum(jnp.max(s, axis=-1, keepdims=True), sink)
                p = jnp.where(valid, jnp.exp(s - m), 0.0)
                denom = jnp.sum(p, axis=-1, keepdims=True) + jnp.exp(sink - m)
                o = jnp.dot(p.astype(BF16), vals, preferred_element_type=F32)
                outs.append(o / denom)
            o_ref[:, pair * LANES:(pair + 1) * LANES] = jnp.where(lane < HD_A, outs[0], outs[1]).astype(BF16)

    kprev_ref[...] = k_rot
    vprev_ref[...] = v_cur
    kk_ref[0] = k_rot
    vk_ref[0] = v_cur


def _attn_prompt(proj, sinks, cos, sin, n_batch, seq):
    nb = seq // WINDOW
    rows = n_batch * seq
    return pl.pallas_call(
        _attn_prompt_kernel,
        grid=(n_batch, nb),
        in_specs=[pl.BlockSpec(memory_space=pltpu.SMEM),
                  pl.BlockSpec((WINDOW, W_A), lambda b, n: (b * nb + n, 0)),
                  pl.BlockSpec((WINDOW, 2 * LANES), lambda b, n: (b * nb + n, C_KA // (2 * LANES))),
                  pl.BlockSpec((WINDOW, LANES), lambda b, n: (n, 0)),
                  pl.BlockSpec((WINDOW, LANES), lambda b, n: (n, 0))],
        out_specs=[pl.BlockSpec((WINDOW, W_A), lambda b, n: (b * nb + n, 0)),
                   pl.BlockSpec((1, WINDOW, LANES), lambda b, n: (b, 0, 0)),
                   pl.BlockSpec((1, WINDOW, LANES), lambda b, n: (b, 0, 0))],
        out_shape=[jax.ShapeDtypeStruct((rows, W_A), BF16),
                   jax.ShapeDtypeStruct((n_batch, WINDOW, LANES), F32),
                   jax.ShapeDtypeStruct((n_batch, WINDOW, LANES), F32)],
        scratch_shapes=[pltpu.VMEM((WINDOW, LANES), F32), pltpu.VMEM((WINDOW, LANES), F32)],
        compiler_params=_cparams(("arbitrary", "arbitrary")),
        name="attn_prompt",
    )(sinks, proj, proj, cos, sin)


def _level_constants():
    t = np.arange(CHUNK)
    w = np.zeros((N_LEVELS + 2, CHUNK, CHUNK), np.float32)
    msk = np.zeros((N_LEVELS + 1, CHUNK, CHUNK), np.float32)
    u = t[None, :]
    tt = t[:, None]
    for lv in range(N_LEVELS):
        m = 1 << lv
        anchor = (tt // (2 * m)) * 2 * m + m - 1
        w[lv] = np.where(tt > anchor, (u > anchor) & (u <= tt), (u > tt) & (u <= anchor))
        msk[lv] = ((tt // (2 * m)) == (u // (2 * m))) & ((tt % (2 * m)) >= m) & ((u % (2 * m)) < m)
    w[N_LEVELS] = u <= tt
    w[N_LEVELS + 1] = u > tt
    msk[N_LEVELS] = np.eye(CHUNK)
    w = w.reshape((N_LEVELS + 2) * CHUNK, CHUNK)
    return (jnp.asarray(np.concatenate([w, w, w], axis=1), BF16), jnp.asarray(msk, F32))


def _decay_sums(la, w_ref):
    hi = la.astype(BF16)
    r = la - hi.astype(F32)
    mid = r.astype(BF16)
    lo = (r - mid.astype(F32)).astype(BF16)
    return jnp.dot(w_ref[...], jnp.concatenate([hi, mid, lo], axis=0), preferred_element_type=F32)


def _chunk_core(q, k, dall, heads, msk_ref):
    def blk(i):
        return dall[i * CHUNK:(i + 1) * CHUNK]

    qs = [(q * jnp.exp(blk(lv))) for lv in range(N_LEVELS)] + [q]
    ks = [(k * jnp.exp(blk(lv))).astype(BF16) for lv in range(N_LEVELS)] + [k.astype(BF16)]
    b = blk(N_LEVELS)
    q_in = q * jnp.exp(b)
    k_out = k * jnp.exp(blk(N_LEVELS + 1))
    decay_end = jnp.exp(b[CHUNK - 1:CHUNK, :])

    outs = []
    for lane_mask, v, st_ref in heads:
        def sel(x):
            return x if lane_mask is None else jnp.where(lane_mask, x, 0.0)

        att = jnp.zeros((CHUNK, CHUNK), F32)
        for lv in range(N_LEVELS + 1):
            r = lax.dot_general(sel(qs[lv]).astype(BF16), ks[lv], (((1,), (1,)), ((), ())),
                                preferred_element_type=F32)
            att = att + msk_ref[lv] * r
        vb = v.astype(BF16)
        st = st_ref[...]
        o = jnp.dot(att.astype(BF16), vb, preferred_element_type=F32)
        o = o + lax.dot_general(sel(q_in).astype(BF16), st.astype(BF16), (((1,), (1,)), ((), ())),
                                preferred_element_type=F32)
        upd = lax.dot_general(vb, sel(k_out).astype(BF16), (((0,), (0,)), ((), ())),
                              preferred_element_type=F32)
        st_ref[...] = st * decay_end + upd
        outs.append(o)
    return outs


def _sigmoid(x):
    return 1.0 / (1.0 + jnp.exp(-x))


def _silu(x):
    return x * _sigmoid(x)


def _log_sigmoid(x):
    return jnp.minimum(x, 0.0) - jnp.log1p(jnp.exp(-jnp.abs(x)))


def _rms_gate(o, w_row, gate):
    return o * lax.rsqrt(jnp.mean(o * o, axis=-1, keepdims=True) + RMS_EPS) * w_row * _silu(gate)


PAIR = 2 * LANES


def _gla_log_decay(a_blk, wa2, ba):
    z = jnp.dot(a_blk.astype(BF16), wa2.astype(BF16), preferred_element_type=F32) + ba
    return _log_sigmoid(z) / GLA_NORMALIZER


def _rec_prompt_kernel(qb_ref, fb_ref, ib_ref, gb_ref, lb_ref, nwb_ref,
                       qc_ref, kc_ref, vc_ref, gc_ref, ac_ref, wa2_ref, ba_ref, nwc_ref, w_ref, msk_ref,
                       ob_ref, oc_ref, sb_ref, sc_ref, stb_ref, stc_ref):
    c = pl.program_id(2)

    @pl.when(c == 0)
    def _():
        stb_ref[...] = jnp.zeros_like(stb_ref)
        stc_ref[...] = jnp.zeros_like(stc_ref)

    lb = lb_ref[0]
    f = lb + (1.0 - lb) * _sigmoid(fb_ref[...])
    la_b = jnp.log(jnp.maximum(f, F_FLOOR))
    la_c = _gla_log_decay(ac_ref[...], wa2_ref[...], ba_ref[0])
    dall = _decay_sums(jnp.concatenate([la_b, la_c], axis=1), w_ref)

    for h in range(2):
        cols = slice(h * LANES, (h + 1) * LANES)
        (o,) = _chunk_core(_silu(qb_ref[:, cols]), 1.0 - f[:, cols], dall[:, cols],
                           [(None, ib_ref[:, cols], stb_ref.at[h])], msk_ref)
        ob_ref[:, cols] = _rms_gate(o, nwb_ref[0, :, cols], gb_ref[:, cols]).astype(BF16)

    lane = lax.broadcasted_iota(jnp.int32, (CHUNK, LANES), 1)
    heads = []
    for half in range(2):
        mask = (lane >= half * DK_C) & (lane < (half + 1) * DK_C)
        heads.append((mask, vc_ref[:, half * DV_C:(half + 1) * DV_C], stc_ref.at[half]))
    outs = _chunk_core(qc_ref[...] * (DK_C ** -0.5), kc_ref[...], dall[:, PAIR:PAIR + LANES], heads, msk_ref)
    for half in range(2):
        hc = slice(half * DV_C, (half + 1) * DV_C)
        oc_ref[:, hc] = _rms_gate(outs[half], nwc_ref[0, :, hc], gc_ref[:, hc]).astype(BF16)

    @pl.when(c == pl.num_programs(2) - 1)
    def _():
        for h in range(2):
            sb_ref[0, h] = stb_ref[h].T
            sc_ref[0, h] = stc_ref[h].T[h * DK_C:(h + 1) * DK_C, :]


def _rec_prompt(proj, lb, nw_b, wa2_pad, ba, nw_c, consts, n_batch, seq):
    nc = seq // CHUNK
    w, msk = consts
    n_pair = H_B // 2
    assert H_C // 2 == n_pair

    def col(width, off):
        return pl.BlockSpec((CHUNK, width), lambda b, p, c: (b * nc + c, off // width + p))

    def full(shape):
        return pl.BlockSpec(shape, lambda b, p, c: (0,) * len(shape))

    def vec(width):
        return pl.BlockSpec((1, 1, width), lambda b, p, c: (p, 0, 0))

    out_rows = pl.BlockSpec((CHUNK, PAIR), lambda b, p, c: (b * nc + c, p))
    return pl.pallas_call(
        _rec_prompt_kernel,
        grid=(n_batch, n_pair, nc),
        in_specs=[col(PAIR, C_QB), col(PAIR, C_FB), col(PAIR, C_IB), col(PAIR, C_GB), vec(PAIR), vec(PAIR),
                  col(LANES, C_QC), col(LANES, C_KC), col(PAIR, C_VC), col(PAIR, C_GC),
                  pl.BlockSpec((CHUNK, LANES), lambda b, p, c: (b * nc + c, C_AC // LANES)),
                  pl.BlockSpec((LANES, LANES), lambda b, p, c: (0, p)), vec(LANES), vec(PAIR),
                  full(w.shape), full(msk.shape)],
        out_specs=[out_rows, out_rows,
                   pl.BlockSpec((1, 2, DK_B, DV_B), lambda b, p, c: (b, p, 0, 0)),
                   pl.BlockSpec((1, 2, DK_C, DV_C), lambda b, p, c: (b, p, 0, 0))],
        out_shape=[jax.ShapeDtypeStruct((n_batch * seq, W_B), BF16),
                   jax.ShapeDtypeStruct((n_batch * seq, W_C), BF16),
                   jax.ShapeDtypeStruct((n_batch, H_B, DK_B, DV_B), F32),
                   jax.ShapeDtypeStruct((n_batch, H_C, DK_C, DV_C), F32)],
        scratch_shapes=[pltpu.VMEM((2, DV_B, DK_B), F32), pltpu.VMEM((2, DV_C, LANES), F32)],
        compiler_params=_cparams(("arbitrary", "arbitrary", "arbitrary")),
        name="rec_prompt",
    )(proj, proj, proj, proj, lb.reshape(n_pair, 1, PAIR), nw_b.reshape(n_pair, 1, PAIR),
      proj, proj, proj, proj, proj, wa2_pad, ba.reshape(n_pair, 1, LANES), nw_c.reshape(n_pair, 1, PAIR),
      w, msk)


def _row_to_col(row):
    n = row.shape[1]
    r = lax.broadcasted_iota(jnp.int32, (n, n), 0)
    c = lax.broadcasted_iota(jnp.int32, (n, n), 1)
    return jnp.sum(jnp.where(r == c, jnp.broadcast_to(row, (n, n)), 0.0), axis=1, keepdims=True)


def _sample_kernel(p_ref, ck_ref, cv_ref, sb_ref, sc_ref, sink_ref, cos_ref, sin_ref, lb_ref, nwb_ref,
                   wa2_ref, ba_ref, nwc_ref,
                   mix_ref, ko_ref, vo_ref, sbo_ref, sco_ref):
    for n in range(p_ref.shape[0]):
        _sample_one(n, p_ref, ck_ref, cv_ref, sb_ref, sc_ref, sink_ref, cos_ref, sin_ref, lb_ref, nwb_ref,
                    wa2_ref, ba_ref, nwc_ref, mix_ref, ko_ref, vo_ref, sbo_ref, sco_ref)


def _sample_one(n, p_ref, ck_ref, cv_ref, sb_ref, sc_ref, sink_ref, cos_ref, sin_ref, lb_ref, nwb_ref,
                wa2_ref, ba_ref, nwc_ref, mix_ref, ko_ref, vo_ref, sbo_ref, sco_ref):
    cos, sin = cos_ref[...], sin_ref[...]
    scale = HD_A ** -0.5

    k_new = _rope128(p_ref[n, :, C_KA:C_KA + LANES], cos, sin)
    v_new = p_ref[n, :, C_VA:C_VA + LANES]
    kc, vc = ck_ref[n], cv_ref[n]
    lane = lax.broadcasted_iota(jnp.int32, (1, LANES), 1)
    heads_per_kv = H_A // KV_A
    head_row = lax.broadcasted_iota(jnp.int32, (heads_per_kv, 1), 0)
    for g in range(KV_A):
        pairs = range(g * heads_per_kv // 2, (g + 1) * heads_per_kv // 2)
        q_rows = []
        sink = jnp.zeros((heads_per_kv, 1), F32)
        for pair in pairs:
            qp = _rope128(p_ref[n, :, pair * LANES:(pair + 1) * LANES], cos, sin) * scale
            for half in range(2):
                sink = jnp.where(head_row == len(q_rows), sink_ref[2 * pair + half], sink)
                q_rows.append(jnp.where((lane >= half * HD_A) & (lane < (half + 1) * HD_A), qp, 0.0))
        qm = jnp.concatenate(q_rows, axis=0)
        kd, vd = _dup_half(kc, g).astype(BF16), _dup_half(vc, g).astype(BF16)
        knd, vnd = _dup_half(k_new, g), _dup_half(v_new, g)
        s_c = lax.dot_general(qm.astype(BF16), kd, (((1,), (1,)), ((), ())), preferred_element_type=F32)
        s_n = jnp.sum(qm * knd, axis=-1, keepdims=True)
        m = jnp.maximum(jnp.maximum(jnp.max(s_c, axis=-1, keepdims=True), s_n), sink)
        p_c, p_n = jnp.exp(s_c - m), jnp.exp(s_n - m)
        denom = jnp.sum(p_c, axis=-1, keepdims=True) + p_n + jnp.exp(sink - m)
        o = (jnp.dot(p_c.astype(BF16), vd, preferred_element_type=F32) + p_n * vnd) / denom
        for j, pair in enumerate(pairs):
            mix_ref[n, :, pair * LANES:(pair + 1) * LANES] = jnp.where(
                lane < HD_A, o[2 * j:2 * j + 1], o[2 * j + 1:2 * j + 2])
    ko_ref[n, 0:WINDOW - 1, :] = kc[1:WINDOW, :]
    ko_ref[n, WINDOW - 1:WINDOW, :] = k_new
    vo_ref[n, 0:WINDOW - 1, :] = vc[1:WINDOW, :]
    vo_ref[n, WINDOW - 1:WINDOW, :] = v_new

    for h in range(H_B):
        cols = slice(h * LANES, (h + 1) * LANES)
        lb = lb_ref[:, cols]
        f = lb + (1.0 - lb) * _sigmoid(p_ref[n, :, C_FB + h * LANES:C_FB + (h + 1) * LANES])
        a = jnp.exp(jnp.log(jnp.maximum(f, F_FLOOR)))
        q = _silu(p_ref[n, :, C_QB + h * LANES:C_QB + (h + 1) * LANES])
        v = p_ref[n, :, C_IB + h * LANES:C_IB + (h + 1) * LANES]
        s_new = _row_to_col(a) * sb_ref[n, h] + _row_to_col(1.0 - f) * v
        sbo_ref[n, h] = s_new
        o = jnp.sum(_row_to_col(q) * s_new, axis=0, keepdims=True)
        gate = p_ref[n, :, C_GB + h * LANES:C_GB + (h + 1) * LANES]
        mix_ref[n, :, W_A + h * LANES:W_A + (h + 1) * LANES] = _rms_gate(o, nwb_ref[:, cols], gate)

    la_all = _gla_log_decay(jnp.broadcast_to(p_ref[n, :, C_AC:C_AC + LANES], (SUBLANES, LANES)),
                            wa2_ref[...], ba_ref[...])[0:1]
    for pair in range(H_C // 2):
        cols = slice(pair * LANES, (pair + 1) * LANES)
        a_col = _row_to_col(jnp.exp(la_all[:, cols]))
        q_col = _row_to_col(p_ref[n, :, C_QC + pair * LANES:C_QC + (pair + 1) * LANES] * (DK_C ** -0.5))
        k_col = _row_to_col(p_ref[n, :, C_KC + pair * LANES:C_KC + (pair + 1) * LANES])
        for half in range(2):
            h = 2 * pair + half
            rows = slice(half * DK_C, (half + 1) * DK_C)
            v = p_ref[n, :, C_VC + h * DV_C:C_VC + (h + 1) * DV_C]
            s_new = a_col[rows] * sc_ref[n, h] + k_col[rows] * v
            sco_ref[n, h] = s_new
            o = jnp.sum(q_col[rows] * s_new, axis=0, keepdims=True)
            gate = p_ref[n, :, C_GC + h * DV_C:C_GC + (h + 1) * DV_C]
            hc = slice(h * DV_C, (h + 1) * DV_C)
            mix_ref[n, :, W_A + W_B + h * DV_C:W_A + W_B + (h + 1) * DV_C] = _rms_gate(o, nwc_ref[:, hc], gate)


def _sample_mixers(proj_s, cache_k, cache_v, s_b, s_c, layer, sinks, cos, sin, lb, nw_b, wa2_pad, ba, nw_c):
    nseq = proj_s.shape[0]
    wqk = H_C * DK_C

    def per_seq(shape):
        return pl.BlockSpec((SEQ_PER_STEP,) + shape, lambda b: (b,) + (0,) * len(shape))

    def per_seq_of_layer(shape):
        return pl.BlockSpec((None, SEQ_PER_STEP) + shape, lambda b: (layer, b) + (0,) * len(shape))

    def full(shape):
        return pl.BlockSpec(shape, lambda b: (0,) * len(shape))

    return pl.pallas_call(
        _sample_kernel,
        grid=(nseq // SEQ_PER_STEP,),
        in_specs=[per_seq((1, PROJ_PAD)), per_seq_of_layer((WINDOW, LANES)), per_seq_of_layer((WINDOW, LANES)),
                  per_seq_of_layer((H_B, DK_B, DV_B)), per_seq_of_layer((H_C, DK_C, DV_C)),
                  pl.BlockSpec(memory_space=pltpu.SMEM),
                  full((1, LANES)), full((1, LANES)), full((1, W_B)), full((1, W_B)),
                  full((LANES, wqk)), full((1, wqk)), full((1, W_C))],
        out_specs=[per_seq((1, D_MODEL)), per_seq((WINDOW, LANES)), per_seq((WINDOW, LANES)),
                   per_seq((H_B, DK_B, DV_B)), per_seq((H_C, DK_C, DV_C))],
        out_shape=[jax.ShapeDtypeStruct((nseq, 1, D_MODEL), F32),
                   jax.ShapeDtypeStruct((nseq, WINDOW, LANES), F32),
                   jax.ShapeDtypeStruct((nseq, WINDOW, LANES), F32),
                   jax.ShapeDtypeStruct((nseq, H_B, DK_B, DV_B), F32),
                   jax.ShapeDtypeStruct((nseq, H_C, DK_C, DV_C), F32)],
        compiler_params=_cparams(("arbitrary",)),
        name="sample_mixers",
    )(proj_s, cache_k, cache_v, s_b, s_c, sinks, cos, sin, lb.reshape(1, W_B), nw_b.reshape(1, W_B),
      wa2_pad, ba.reshape(1, wqk), nw_c.reshape(1, W_C))


N_TOP = PEER_TOPK + 1
TOP_ROWS = 24


SORT_N = 16


def _top_values(x, count):
    groups = [x[r:r + SUBLANES] for r in range(0, x.shape[0], SUBLANES)]
    groups += [jnp.full((SUBLANES, x.shape[1]), NEG_INF, F32)] * (SORT_N - len(groups))
    k = 2
    while k <= SORT_N:
        j = k // 2
        while j >= 1:
            for i in range(SORT_N):
                l = i ^ j
                if l > i:
                    hi, lo = jnp.maximum(groups[i], groups[l]), jnp.minimum(groups[i], groups[l])
                    groups[i], groups[l] = (hi, lo) if (i & k) == 0 else (lo, hi)
            j //= 2
        k *= 2
    out = []
    for r in range(count):
        m = jnp.max(groups[0], axis=0, keepdims=True)
        out.append(m)
        hit = groups[0] >= m
        depth = min(SORT_N, count - r)
        for d in range(depth - 1):
            groups[d] = jnp.where(hit, groups[d + 1], groups[d])
        groups[depth - 1] = jnp.where(hit, NEG_INF, groups[depth - 1])
    return out


def _top_desc(cur, dst_ref):
    dst_ref[...] = jnp.full(dst_ref.shape, NEG_INF, F32)
    for r, m in enumerate(_top_values(cur, N_TOP)):
        dst_ref[r:r + 1, :] = m


def _route_kernel(q_ref, keys_ref, s0_ref, s1_ref, ea_ref, eb_ref, thr_ref, sc_ref, v0_ref, v1_ref):
    tt = q_ref.shape[0]
    for p in range(2):
        qh = q_ref[:, p * LANES:(p + 1) * LANES].astype(BF16)
        sc_ref[p] = lax.dot_general(keys_ref[0, p].astype(BF16), qh, (((1,), (1,)), ((), ())),
                                    preferred_element_type=F32)

    def chunk(c, carry):
        cols = pl.ds(pl.multiple_of(c * LANES, LANES), LANES)
        s0, s1 = sc_ref[0, :, cols], sc_ref[1, :, cols]
        _top_desc(s0, v0_ref)
        _top_desc(s1, v1_ref)
        v0, v1 = v0_ref[...], v1_ref[...]
        cands = jnp.concatenate(
            [v0[0:1] + v1] + [v0[a:a + 1] + v1[0:8] for a in range(1, 8)] + [v0[8:TOP_ROWS] + v1[0:1]], axis=0)
        best = _top_values(cands, N_TOP)
        kth, below = best[PEER_TOPK - 1], best[PEER_TOPK]
        cut = jnp.where(below == NEG_INF, kth, 0.5 * (kth + below))
        top = v0[0:1] + v1[0:1]
        z = jnp.sum(jnp.where(cands >= kth, jnp.exp(cands - top), 0.0), axis=0, keepdims=True)
        s0_ref[0, :, cols] = s0
        s1_ref[0, :, cols] = s1
        ea_ref[0, :, cols] = jnp.exp(s0 - v0[0:1]) / z
        eb_ref[0, :, cols] = jnp.exp(s1 - v1[0:1])
        thr_ref[0, :, cols] = cut
        return carry

    lax.fori_loop(0, tt // LANES, chunk, 0)


def _peer_route(qp, keys, layer, tt=TOK_TILE):
    t = qp.shape[0]
    big = pl.BlockSpec((1, N_KEYS, tt), lambda i, h: (h, 0, i))
    big_shape = jax.ShapeDtypeStruct((PEER_HEADS, N_KEYS, t), F32)
    return pl.pallas_call(
        _route_kernel,
        grid=(t // tt, PEER_HEADS),
        in_specs=[pl.BlockSpec((tt, 2 * LANES), lambda i, h: (i, h)),
                  pl.BlockSpec((None, 1, 2, N_KEYS, LANES), lambda i, h: (layer, h, 0, 0, 0))],
        out_specs=[big, big, big, big, pl.BlockSpec((1, 1, tt), lambda i, h: (h, 0, i))],
        out_shape=[big_shape, big_shape, big_shape, big_shape,
                   jax.ShapeDtypeStruct((PEER_HEADS, 1, t), F32)],
        scratch_shapes=[pltpu.VMEM((2, N_KEYS, tt), F32), pltpu.VMEM((TOP_ROWS, LANES), F32),
                        pltpu.VMEM((TOP_ROWS, LANES), F32)],
        compiler_params=_cparams(("arbitrary", "arbitrary")),
        name="peer_route",
    )(qp, keys)


def _gelu(x):
    return 0.5 * x * (1.0 + lax.erf(x * (2.0 ** -0.5)))


def _peer_dense_kernel(x_ref, u_ref, v_ref, s0_ref, s1_ref, ea_ref, eb_ref, cut_ref, res_ref, g_ref, b_ref,
                       y_ref, yb_ref, o_ref):
    e = pl.program_id(1)
    tt = x_ref.shape[0]

    @pl.when(e == 0)
    def _():
        o_ref[...] = jnp.zeros_like(o_ref)

    def step(row0):
        hid = lax.dot_general(u_ref[...], x_ref[...], (((1,), (1,)), ((), ())), preferred_element_type=F32)
        blocks = []
        for il in range(I_PER_TILE):
            rows = slice(il * N_KEYS, (il + 1) * N_KEYS)
            parts = []
            for c in range(tt // LANES):
                cols = slice(c * LANES, (c + 1) * LANES)
                g = jnp.zeros((N_KEYS, LANES), F32)
                for h in range(PEER_HEADS):
                    need = cut_ref[h, :, cols] - s0_ref[h, row0 + il:row0 + il + 1, cols]
                    gate = eb_ref[h, :, cols] * ea_ref[h, row0 + il:row0 + il + 1, cols]
                    g = g + jnp.where(s1_ref[h, :, cols] >= need, gate, 0.0)
                parts.append((g * _gelu(hid[rows, cols])).astype(BF16))
            blocks.append(jnp.concatenate(parts, axis=1))
        w = jnp.concatenate(blocks, axis=0)
        o_ref[...] += lax.dot_general(w, v_ref[...], (((0,), (0,)), ((), ())), preferred_element_type=F32)

    @pl.when(e % 2 == 0)
    def _():
        step(0)

    @pl.when(e % 2 == 1)
    def _():
        step(I_PER_TILE)

    @pl.when(e == pl.num_programs(1) - 1)
    def _():
        out = _res_ln(res_ref[...], o_ref[...], g_ref[...], b_ref[...])
        y_ref[...] = out
        yb_ref[...] = out.astype(BF16)


def _peer_dense_res_layernorm(xb, u, v, layer, s0, s1, ea, eb, thr, res, g, b, tt=TOK_TILE):
    t, d = xb.shape
    n_tiles = u.shape[1] // EXP_TILE
    once = pl.Buffered(1)
    big = pl.BlockSpec((PEER_HEADS, N_KEYS, tt), lambda i, e: (0, 0, i), pipeline_mode=once)
    rows = pl.BlockSpec((PEER_HEADS, 2 * I_PER_TILE, tt), lambda i, e: (0, e // 2, i))
    tok = pl.BlockSpec((tt, d), lambda i, e: (i, 0))
    vec = pl.BlockSpec((1, d), lambda i, e: (0, 0))
    return pl.pallas_call(
        _peer_dense_kernel,
        grid=(t // tt, n_tiles),
        in_specs=[pl.BlockSpec((tt, d), lambda i, e: (i, 0), pipeline_mode=once),
                  pl.BlockSpec((None, EXP_TILE, d), lambda i, e: (layer, e, 0)),
                  pl.BlockSpec((None, EXP_TILE, d), lambda i, e: (layer, e, 0)),
                  rows, big, rows, big,
                  pl.BlockSpec((PEER_HEADS, 1, tt), lambda i, e: (0, 0, i), pipeline_mode=once),
                  pl.BlockSpec((tt, d), lambda i, e: (i, 0), pipeline_mode=once), vec, vec],
        out_specs=[tok, tok],
        out_shape=[jax.ShapeDtypeStruct((t, d), F32), jax.ShapeDtypeStruct((t, d), BF16)],
        scratch_shapes=[pltpu.VMEM((tt, d), F32)],
        compiler_params=_cparams(("arbitrary", "arbitrary")),
        name="peer_dense",
    )(xb, u, v, s0, s1, ea, eb, thr, res, g.reshape(1, d), b.reshape(1, d))


def kernel(x_prompt, x_sample, cache_k, cache_v, state_hgrn, state_gla, w_in, w_out, attn_sinks,
           hgrn_norm_w, lb_logits, gla_wa2, gla_ba, gla_norm_w, ln1_g, ln1_b, ln2_g, ln2_b,
           peer_wq, peer_keys, peer_u, peer_v):
    n_batch, seq, d = x_prompt.shape
    n_dec = x_sample.shape[0]
    t_prompt = n_batch * seq
    t_real = t_prompt + n_dec
    tok_pad = -(-t_real // TOK_TILE) * TOK_TILE
    past_len = 16384

    sm = jax.nn.softmax(lb_logits.astype(F32), axis=0)
    lower = jnp.cumsum(sm, axis=0) - sm[0:1]
    cos_p, sin_p = _rope_tables(jnp.arange(seq, dtype=jnp.int32))
    cos_s, sin_s = _rope_tables(past_len + jnp.arange(1, dtype=jnp.int32))
    consts = _level_constants()
    u_bf, v_bf = peer_u.astype(BF16), peer_v.astype(BF16)
    w_in_t = jnp.swapaxes(w_in, 1, 2)
    cache_k2 = cache_k.reshape(DEPTH, n_dec, WINDOW, LANES)
    cache_v2 = cache_v.reshape(DEPTH, n_dec, WINDOW, LANES)

    x = jnp.concatenate([x_prompt.reshape(t_prompt, d), x_sample.reshape(n_dec, d),
                         jnp.zeros((tok_pad - t_real, d), F32)], axis=0)
    xb = x.astype(BF16)
    pad_rows = jnp.zeros((tok_pad - t_real, d), BF16)

    outs = [[] for _ in range(8)]
    for l in range(DEPTH):
        wa2_pad = jnp.zeros((LANES, H_C * DK_C), F32).at[:GLA_RANK].set(gla_wa2[l])
        proj = _matmul_nt(xb, w_in_t, l, PROJ_PAD, tm=768, tn=1664)
        o_a, k_keep, v_keep = _attn_prompt(proj, attn_sinks[l], cos_p, sin_p, n_batch, seq)
        o_b, o_c, s_b, s_c = _rec_prompt(proj, lower[l], hgrn_norm_w[l], wa2_pad, gla_ba[l], gla_norm_w[l],
                                         consts, n_batch, seq)
        proj_s = proj[t_prompt:t_real].reshape(n_dec, 1, PROJ_PAD)
        mix_s, k_s, v_s, sb_s, sc_s = _sample_mixers(
            proj_s, cache_k2, cache_v2, state_hgrn, state_gla, l, attn_sinks[l], cos_s, sin_s, lower[l],
            hgrn_norm_w[l], wa2_pad, gla_ba[l], gla_norm_w[l])
        mix = jnp.concatenate([jnp.concatenate([o_a, o_b, o_c], axis=1), mix_s.reshape(n_dec, d).astype(BF16), pad_rows],
                              axis=0)
        x1, x1b = _matmul_res_layernorm(mix, w_out, l, x, ln1_g[l], ln1_b[l])
        qp = _matmul(x1b, peer_wq, l, tm=768, tn=1024)
        s0, s1, ea, eb, thr = _peer_route(qp, peer_keys, l)
        x, xb = _peer_dense_res_layernorm(x1b, u_bf, v_bf, l, s0, s1, ea, eb, thr, x1, ln2_g[l], ln2_b[l])
        for lst, val in zip(outs, (k_keep.reshape(n_batch, WINDOW, KV_A, HD_A),
                                   v_keep.reshape(n_batch, WINDOW, KV_A, HD_A), s_b, s_c,
                                   k_s.reshape(n_dec, WINDOW, KV_A, HD_A),
                                   v_s.reshape(n_dec, WINDOW, KV_A, HD_A), sb_s, sc_s)):
            lst.append(val)

    y_prompt = x[:t_prompt].reshape(n_batch, seq, d)
    y_sample = x[t_prompt:t_real].reshape(n_dec, 1, d)
    return (y_prompt, y_sample) + tuple(jnp.stack(o) for o in outs)
```

```python
import functools
import math

import jax
import jax.numpy as jnp
import numpy as np
from jax import lax
from jax.experimental import pallas as pl
from jax.experimental.pallas import tpu as pltpu

F32 = jnp.float32
BF16 = jnp.bfloat16

D_MODEL = 2048
DEPTH = 4
H_A, HD_A, KV_A = 16, 64, 2
WINDOW = 128
ROPE_THETA = 10000.0
H_B, DK_B, DV_B = 4, 128, 128
H_C, DK_C, DV_C = 4, 64, 128
GLA_RANK = 16
GLA_NORMALIZER = 16.0
W_A, W_B, W_C = H_A * HD_A, H_B * DV_B, H_C * DV_C
N_KEYS = 128
N_EXPERTS = N_KEYS * N_KEYS
PEER_HEADS = 8
PEER_TOPK = 16
ALPHA = (2 * DEPTH) ** 0.25
LN_EPS = 1e-5
RMS_EPS = 1e-6
MASK_VALUE = -1e30
F_FLOOR = 1e-30
NEG_INF = float("-inf")

C_QA, C_KA, C_VA = 0, 1024, 1152
C_QB, C_FB, C_IB, C_GB = 1280, 1792, 2304, 2816
C_QC, C_KC, C_VC, C_GC, C_AC = 3328, 3584, 3840, 4352, 4864
PROJ_COLS = 4880

LANES = 128
SUBLANES = 8
VMEM_LIMIT = 56 * 1024 * 1024
PROJ_PAD = 4992
CHUNK = 128
N_LEVELS = 7
TOK_TILE = 768
EXP_TILE = 512
I_PER_TILE = EXP_TILE // N_KEYS
SEQ_PER_STEP = 4


def _cparams(sem):
    return pltpu.CompilerParams(dimension_semantics=sem, vmem_limit_bytes=VMEM_LIMIT)


def _mm_kernel(a_ref, w_ref, o_ref, wb_ref):
    @pl.when(pl.program_id(1) == 0)
    def _():
        wb_ref[...] = w_ref[...].astype(BF16)

    o_ref[...] = jnp.dot(a_ref[...], wb_ref[...], preferred_element_type=F32)


def _matmul(a, w, layer, tm, tn):
    m, k = a.shape
    n = w.shape[2]
    return pl.pallas_call(
        _mm_kernel,
        grid=(n // tn, m // tm),
        in_specs=[pl.BlockSpec((tm, k), lambda j, i: (i, 0)),
                  pl.BlockSpec((None, k, tn), lambda j, i: (layer, 0, j))],
        out_specs=pl.BlockSpec((tm, tn), lambda j, i: (i, j)),
        out_shape=jax.ShapeDtypeStruct((m, n), F32),
        scratch_shapes=[pltpu.VMEM((k, tn), BF16)],
        compiler_params=_cparams(("arbitrary", "arbitrary")),
        name="matmul",
    )(a, w)


def _mm_nt_kernel(n_valid, a_ref, wt_ref, o_ref, wb_ref):
    j = pl.program_id(0)
    tn = o_ref.shape[1]

    @pl.when(pl.program_id(1) == 0)
    def _():
        wb_ref[...] = wt_ref[...].astype(BF16)

    out = lax.dot_general(a_ref[...], wb_ref[...], (((1,), (1,)), ((), ())), preferred_element_type=F32)
    col = j * tn + lax.broadcasted_iota(jnp.int32, out.shape, 1)
    o_ref[...] = jnp.where(col < n_valid, out, 0.0)


def _matmul_nt(a, wt, layer, n_out, tm, tn):
    m, k = a.shape
    return pl.pallas_call(
        functools.partial(_mm_nt_kernel, wt.shape[1]),
        grid=(n_out // tn, m // tm),
        in_specs=[pl.BlockSpec((tm, k), lambda j, i: (i, 0)),
                  pl.BlockSpec((None, tn, k), lambda j, i: (layer, j, 0))],
        out_specs=pl.BlockSpec((tm, tn), lambda j, i: (i, j)),
        out_shape=jax.ShapeDtypeStruct((m, n_out), F32),
        scratch_shapes=[pltpu.VMEM((tn, k), BF16)],
        compiler_params=_cparams(("arbitrary", "arbitrary")),
        name="matmul_nt",
    )(a, wt)


def _res_ln(x, y, g, b):
    z = ALPHA * x + y
    mu = jnp.mean(z, axis=-1, keepdims=True)
    zc = z - mu
    var = jnp.mean(zc * zc, axis=-1, keepdims=True)
    return zc * lax.rsqrt(var + LN_EPS) * g + b


def _mm_ln_kernel(n_head_tiles, oa_ref, ob_ref, oc_ref, tail_ref, w_ref, x_ref, g_ref, b_ref,
                  o_ref, obf_ref, wb_ref, a_ref):
    i = pl.program_id(0)

    @pl.when(i == 0)
    def _():
        wb_ref[...] = w_ref[...].astype(BF16)

    @pl.when(i < n_head_tiles)
    def _():
        a_ref[:, 0:W_A] = oa_ref[...]
        a_ref[:, W_A:W_A + W_B] = ob_ref[...]
        a_ref[:, W_A + W_B:W_A + W_B + W_C] = oc_ref[...]

    @pl.when(i >= n_head_tiles)
    def _():
        a_ref[...] = tail_ref[...]

    y = jnp.dot(a_ref[...], wb_ref[...], preferred_element_type=F32)
    out = _res_ln(x_ref[...], y, g_ref[...], b_ref[...])
    o_ref[...] = out
    obf_ref[...] = out.astype(BF16)


def _out_proj_res_layernorm(o_a, o_b, o_c, tail, w, layer, x, g, b, tm=256):
    m, d = x.shape
    k = w.shape[1]
    n_head = o_a.shape[0] // tm
    assert o_a.shape[0] % tm == 0 and tail.shape[0] % tm == 0 and o_a.shape[0] + tail.shape[0] == m
    row = pl.BlockSpec((tm, d), lambda i: (i, 0))
    vec = pl.BlockSpec((1, d), lambda i: (0, 0))

    def head(width):
        return pl.BlockSpec((tm, width), lambda i: (jnp.minimum(i, n_head - 1), 0))

    return pl.pallas_call(
        functools.partial(_mm_ln_kernel, n_head),
        grid=(m // tm,),
        in_specs=[head(W_A), head(W_B), head(W_C),
                  pl.BlockSpec((tm, k), lambda i: (jnp.maximum(i - n_head, 0), 0)),
                  pl.BlockSpec((None, k, d), lambda i: (layer, 0, 0), pipeline_mode=pl.Buffered(1)),
                  row, vec, vec],
        out_specs=[row, row],
        out_shape=[jax.ShapeDtypeStruct((m, d), F32), jax.ShapeDtypeStruct((m, d), BF16)],
        scratch_shapes=[pltpu.VMEM((k, d), BF16), pltpu.VMEM((tm, k), BF16)],
        compiler_params=_cparams(("arbitrary",)),
        name="out_proj_res_layernorm",
    )(o_a, o_b, o_c, tail, w, x, g.reshape(1, d), b.reshape(1, d))


def _rope_tables(pos):
    half = HD_A // 2
    inv = ROPE_THETA ** (-jnp.arange(half, dtype=F32) / half)
    ang = pos.astype(F32)[:, None] * inv[None, :]
    cos, sin = jnp.cos(ang), jnp.sin(ang)
    return jnp.tile(cos, (1, 4)), jnp.tile(jnp.concatenate([-sin, sin], axis=1), (1, 2))


def _rope128(x, cos, sin_signed):
    lane = lax.broadcasted_iota(jnp.int32, x.shape, 1)
    first = (lane % HD_A) < (HD_A // 2)
    partner = jnp.where(first, pltpu.roll(x, LANES - 32, 1), pltpu.roll(x, 32, 1))
    return x * cos + partner * sin_signed


def _dup_half(x, g):
    lane = lax.broadcasted_iota(jnp.int32, x.shape, 1)
    keep = (lane >= g * HD_A) & (lane < (g + 1) * HD_A)
    return jnp.where(keep, x, pltpu.roll(x, HD_A, 1))


def _attn_prompt_kernel(sink_ref, q_ref, kv_ref, cos_ref, sin_ref,
                        o_ref, kk_ref, vk_ref, kprev_ref, vprev_ref):
    n = pl.program_id(1)

    @pl.when(n == 0)
    def _():
        kprev_ref[...] = jnp.zeros_like(kprev_ref)
        vprev_ref[...] = jnp.zeros_like(vprev_ref)

    cos, sin = cos_ref[...], sin_ref[...]
    k_rot = _rope128(kv_ref[:, 0:LANES], cos, sin)
    v_cur = kv_ref[:, LANES:2 * LANES]
    k_prev, v_prev = kprev_ref[...], vprev_ref[...]

    qi = lax.broadcasted_iota(jnp.int32, (WINDOW, 2 * WINDOW), 0)
    kj = lax.broadcasted_iota(jnp.int32, (WINDOW, 2 * WINDOW), 1)
    first_key = jnp.where(n > 0, 0, WINDOW)
    valid = (kj >= jnp.maximum(qi, first_key)) & (kj <= qi + WINDOW)
    lane = lax.broadcasted_iota(jnp.int32, (WINDOW, LANES), 1)

    for g in range(KV_A):
        keys = jnp.concatenate([_dup_half(k_prev, g), _dup_half(k_rot, g)], axis=0).astype(BF16)
        vals = jnp.concatenate([_dup_half(v_prev, g), _dup_half(v_cur, g)], axis=0).astype(BF16)
        for pair in range(g * 4, g * 4 + 4):
            qp = _rope128(q_ref[:, pair * LANES:(pair + 1) * LANES], cos, sin) * (HD_A ** -0.5)
            outs = []
            for half in range(2):
                sink = sink_ref[2 * pair + half]
                qm = jnp.where((lane >= half * HD_A) & (lane < (half + 1) * HD_A), qp, 0.0).astype(BF16)
                s = lax.dot_general(qm, keys, (((1,), (1,)), ((), ())), preferred_element_type=F32)
                s = jnp.where(valid, s, MASK_VALUE)
                m = jnp.maximum(jnp.max(s, axis=-1, keepdims=True), sink)
                p = jnp.where(valid, jnp.exp(s - m), 0.0)
                denom = jnp.sum(p, axis=-1, keepdims=True) + jnp.exp(sink - m)
                o = jnp.dot(p.astype(BF16), vals, preferred_element_type=F32)
                outs.append(o / denom)
            o_ref[:, pair * LANES:(pair + 1) * LANES] = jnp.where(lane < HD_A, outs[0], outs[1]).astype(BF16)

    kprev_ref[...] = k_rot
    vprev_ref[...] = v_cur
    kk_ref[0] = k_rot
    vk_ref[0] = v_cur


def _attn_prompt(proj, sinks, cos, sin, n_batch, seq):
    nb = seq // WINDOW
    rows = n_batch * seq
    return pl.pallas_call(
        _attn_prompt_kernel,
        grid=(n_batch, nb),
        in_specs=[pl.BlockSpec(memory_space=pltpu.SMEM),
                  pl.BlockSpec((WINDOW, W_A), lambda b, n: (b * nb + n, 0)),
                  pl.BlockSpec((WINDOW, 2 * LANES), lambda b, n: (b * nb + n, C_KA // (2 * LANES))),
                  pl.BlockSpec((WINDOW, LANES), lambda b, n: (n, 0)),
                  pl.BlockSpec((WINDOW, LANES), lambda b, n: (n, 0))],
        out_specs=[pl.BlockSpec((WINDOW, W_A), lambda b, n: (b * nb + n, 0)),
                   pl.BlockSpec((1, WINDOW, LANES), lambda b, n: (b, 0, 0)),
                   pl.BlockSpec((1, WINDOW, LANES), lambda b, n: (b, 0, 0))],
        out_shape=[jax.ShapeDtypeStruct((rows, W_A), BF16),
                   jax.ShapeDtypeStruct((n_batch, WINDOW, LANES), F32),
                   jax.ShapeDtypeStruct((n_batch, WINDOW, LANES), F32)],
        scratch_shapes=[pltpu.VMEM((WINDOW, LANES), F32), pltpu.VMEM((WINDOW, LANES), F32)],
        compiler_params=_cparams(("arbitrary", "arbitrary")),
        name="attn_prompt",
    )(sinks, proj, proj, cos, sin)


def _level_constants():
    t = np.arange(CHUNK)
    w = np.zeros((N_LEVELS + 2, CHUNK, CHUNK), np.float32)
    msk = np.zeros((N_LEVELS + 1, CHUNK, CHUNK), np.float32)
    u = t[None, :]
    tt = t[:, None]
    for lv in range(N_LEVELS):
        m = 1 << lv
        anchor = (tt // (2 * m)) * 2 * m + m - 1
        w[lv] = np.where(tt > anchor, (u > anchor) & (u <= tt), (u > tt) & (u <= anchor))
        msk[lv] = ((tt // (2 * m)) == (u // (2 * m))) & ((tt % (2 * m)) >= m) & ((u % (2 * m)) < m)
    w[N_LEVELS] = u <= tt
    w[N_LEVELS + 1] = u > tt
    msk[N_LEVELS] = np.eye(CHUNK)
    w = w.reshape((N_LEVELS + 2) * CHUNK, CHUNK)
    return (jnp.asarray(np.concatenate([w, w, w], axis=1), BF16), jnp.asarray(msk, F32))


def _decay_sums(la, w_ref):
    hi = la.astype(BF16)
    r = la - hi.astype(F32)
    mid = r.astype(BF16)
    lo = (r - mid.astype(F32)).astype(BF16)
    return jnp.dot(w_ref[...], jnp.concatenate([hi, mid, lo], axis=0), preferred_element_type=F32)


def _chunk_core(q, k, dall, heads, msk_ref):
    def blk(i):
        return dall[i * CHUNK:(i + 1) * CHUNK]

    qs = [(q * jnp.exp(blk(lv))) for lv in range(N_LEVELS)] + [q]
    ks = [(k * jnp.exp(blk(lv))).astype(BF16) for lv in range(N_LEVELS)] + [k.astype(BF16)]
    b = blk(N_LEVELS)
    q_in = q * jnp.exp(b)
    k_out = k * jnp.exp(blk(N_LEVELS + 1))
    decay_end = jnp.exp(b[CHUNK - 1:CHUNK, :])

    outs = []
    for lane_mask, v, st_ref in heads:
        def sel(x):
            return x if lane_mask is None else jnp.where(lane_mask, x, 0.0)

        att = jnp.zeros((CHUNK, CHUNK), F32)
        for lv in range(N_LEVELS + 1):
            r = lax.dot_general(sel(qs[lv]).astype(BF16), ks[lv], (((1,), (1,)), ((), ())),
                                preferred_element_type=F32)
            att = att + msk_ref[lv] * r
        vb = v.astype(BF16)
        st = st_ref[...]
        o = jnp.dot(att.astype(BF16), vb, preferred_element_type=F32)
        o = o + lax.dot_general(sel(q_in).astype(BF16), st.astype(BF16), (((1,), (1,)), ((), ())),
                                preferred_element_type=F32)
        upd = lax.dot_general(vb, sel(k_out).astype(BF16), (((0,), (0,)), ((), ())),
                              preferred_element_type=F32)
        st_ref[...] = st * decay_end + upd
        outs.append(o)
    return outs


def _sigmoid(x):
    return 1.0 / (1.0 + jnp.exp(-x))


def _silu(x):
    return x * _sigmoid(x)


def _log_sigmoid(x):
    return jnp.minimum(x, 0.0) - jnp.log1p(jnp.exp(-jnp.abs(x)))


def _rms_gate(o, w_row, gate):
    return o * lax.rsqrt(jnp.mean(o * o, axis=-1, keepdims=True) + RMS_EPS) * w_row * _silu(gate)


PAIR = 2 * LANES


def _gla_log_decay(a_blk, wa2, ba):
    z = jnp.dot(a_blk.astype(BF16), wa2.astype(BF16), preferred_element_type=F32) + ba
    return _log_sigmoid(z) / GLA_NORMALIZER


def _rec_prompt_kernel(qb_ref, fb_ref, ib_ref, gb_ref, lb_ref, nwb_ref,
                       qc_ref, kc_ref, vc_ref, gc_ref, ac_ref, wa2_ref, ba_ref, nwc_ref, w_ref, msk_ref,
                       ob_ref, oc_ref, sb_ref, sc_ref, stb_ref, stc_ref):
    c = pl.program_id(2)

    @pl.when(c == 0)
    def _():
        stb_ref[...] = jnp.zeros_like(stb_ref)
        stc_ref[...] = jnp.zeros_like(stc_ref)

    lb = lb_ref[0]
    f = lb + (1.0 - lb) * _sigmoid(fb_ref[...])
    la_b = jnp.log(jnp.maximum(f, F_FLOOR))
    la_c = _gla_log_decay(ac_ref[...], wa2_ref[...], ba_ref[0])
    dall = _decay_sums(jnp.concatenate([la_b, la_c], axis=1), w_ref)

    for h in range(2):
        cols = slice(h * LANES, (h + 1) * LANES)
        (o,) = _chunk_core(_silu(qb_ref[:, cols]), 1.0 - f[:, cols], dall[:, cols],
                           [(None, ib_ref[:, cols], stb_ref.at[h])], msk_ref)
        ob_ref[:, cols] = _rms_gate(o, nwb_ref[0, :, cols], gb_ref[:, cols]).astype(BF16)

    lane = lax.broadcasted_iota(jnp.int32, (CHUNK, LANES), 1)
    heads = []
    for half in range(2):
        mask = (lane >= half * DK_C) & (lane < (half + 1) * DK_C)
        heads.append((mask, vc_ref[:, half * DV_C:(half + 1) * DV_C], stc_ref.at[half]))
    outs = _chunk_core(qc_ref[...] * (DK_C ** -0.5), kc_ref[...], dall[:, PAIR:PAIR + LANES], heads, msk_ref)
    for half in range(2):
        hc = slice(half * DV_C, (half + 1) * DV_C)
        oc_ref[:, hc] = _rms_gate(outs[half], nwc_ref[0, :, hc], gc_ref[:, hc]).astype(BF16)

    @pl.when(c == pl.num_programs(2) - 1)
    def _():
        for h in range(2):
            sb_ref[0, h] = stb_ref[h].T
            sc_ref[0, h] = stc_ref[h].T[h * DK_C:(h + 1) * DK_C, :]


def _rec_prompt(proj, lb, nw_b, wa2_pad, ba, nw_c, consts, n_batch, seq):
    nc = seq // CHUNK
    w, msk = consts
    n_pair = H_B // 2
    assert H_C // 2 == n_pair

    def col(width, off):
        return pl.BlockSpec((CHUNK, width), lambda b, p, c: (b * nc + c, off // width + p))

    def full(shape):
        return pl.BlockSpec(shape, lambda b, p, c: (0,) * len(shape))

    def vec(width):
        return pl.BlockSpec((1, 1, width), lambda b, p, c: (p, 0, 0))

    out_rows = pl.BlockSpec((CHUNK, PAIR), lambda b, p, c: (b * nc + c, p))
    return pl.pallas_call(
        _rec_prompt_kernel,
        grid=(n_batch, n_pair, nc),
        in_specs=[col(PAIR, C_QB), col(PAIR, C_FB), col(PAIR, C_IB), col(PAIR, C_GB), vec(PAIR), vec(PAIR),
                  col(LANES, C_QC), col(LANES, C_KC), col(PAIR, C_VC), col(PAIR, C_GC),
                  pl.BlockSpec((CHUNK, LANES), lambda b, p, c: (b * nc + c, C_AC // LANES)),
                  pl.BlockSpec((LANES, LANES), lambda b, p, c: (0, p)), vec(LANES), vec(PAIR),
                  full(w.shape), full(msk.shape)],
        out_specs=[out_rows, out_rows,
                   pl.BlockSpec((1, 2, DK_B, DV_B), lambda b, p, c: (b, p, 0, 0)),
                   pl.BlockSpec((1, 2, DK_C, DV_C), lambda b, p, c: (b, p, 0, 0))],
        out_shape=[jax.ShapeDtypeStruct((n_batch * seq, W_B), BF16),
                   jax.ShapeDtypeStruct((n_batch * seq, W_C), BF16),
                   jax.ShapeDtypeStruct((n_batch, H_B, DK_B, DV_B), F32),
                   jax.ShapeDtypeStruct((n_batch, H_C, DK_C, DV_C), F32)],
        scratch_shapes=[pltpu.VMEM((2, DV_B, DK_B), F32), pltpu.VMEM((2, DV_C, LANES), F32)],
        compiler_params=_cparams(("arbitrary", "arbitrary", "arbitrary")),
        name="rec_prompt",
    )(proj, proj, proj, proj, lb.reshape(n_pair, 1, PAIR), nw_b.reshape(n_pair, 1, PAIR),
      proj, proj, proj, proj, proj, wa2_pad, ba.reshape(n_pair, 1, LANES), nw_c.reshape(n_pair, 1, PAIR),
      w, msk)


def _row_to_col(row):
    n = row.shape[1]
    r = lax.broadcasted_iota(jnp.int32, (n, n), 0)
    c = lax.broadcasted_iota(jnp.int32, (n, n), 1)
    return jnp.sum(jnp.where(r == c, jnp.broadcast_to(row, (n, n)), 0.0), axis=1, keepdims=True)


def _sample_kernel(p_ref, ck_ref, cv_ref, sb_ref, sc_ref, sink_ref, cos_ref, sin_ref, lb_ref, nwb_ref,
                   wa2_ref, ba_ref, nwc_ref,
                   mix_ref, ko_ref, vo_ref, sbo_ref, sco_ref):
    for n in range(p_ref.shape[0]):
        _sample_one(n, p_ref, ck_ref, cv_ref, sb_ref, sc_ref, sink_ref, cos_ref, sin_ref, lb_ref, nwb_ref,
                    wa2_ref, ba_ref, nwc_ref, mix_ref, ko_ref, vo_ref, sbo_ref, sco_ref)


def _sample_one(n, p_ref, ck_ref, cv_ref, sb_ref, sc_ref, sink_ref, cos_ref, sin_ref, lb_ref, nwb_ref,
                wa2_ref, ba_ref, nwc_ref, mix_ref, ko_ref, vo_ref, sbo_ref, sco_ref):
    cos, sin = cos_ref[...], sin_ref[...]
    scale = HD_A ** -0.5

    k_new = _rope128(p_ref[n, :, C_KA:C_KA + LANES], cos, sin)
    v_new = p_ref[n, :, C_VA:C_VA + LANES]
    kc, vc = ck_ref[n], cv_ref[n]
    lane = lax.broadcasted_iota(jnp.int32, (1, LANES), 1)
    heads_per_kv = H_A // KV_A
    head_row = lax.broadcasted_iota(jnp.int32, (heads_per_kv, 1), 0)
    for g in range(KV_A):
        pairs = range(g * heads_per_kv // 2, (g + 1) * heads_per_kv // 2)
        q_rows = []
        sink = jnp.zeros((heads_per_kv, 1), F32)
        for pair in pairs:
            qp = _rope128(p_ref[n, :, pair * LANES:(pair + 1) * LANES], cos, sin) * scale
            for half in range(2):
                sink = jnp.where(head_row == len(q_rows), sink_ref[2 * pair + half], sink)
                q_rows.append(jnp.where((lane >= half * HD_A) & (lane < (half + 1) * HD_A), qp, 0.0))
        qm = jnp.concatenate(q_rows, axis=0)
        kd, vd = _dup_half(kc, g).astype(BF16), _dup_half(vc, g).astype(BF16)
        knd, vnd = _dup_half(k_new, g), _dup_half(v_new, g)
        s_c = lax.dot_general(qm.astype(BF16), kd, (((1,), (1,)), ((), ())), preferred_element_type=F32)
        s_n = jnp.sum(qm * knd, axis=-1, keepdims=True)
        m = jnp.maximum(jnp.maximum(jnp.max(s_c, axis=-1, keepdims=True), s_n), sink)
        p_c, p_n = jnp.exp(s_c - m), jnp.exp(s_n - m)
        denom = jnp.sum(p_c, axis=-1, keepdims=True) + p_n + jnp.exp(sink - m)
        o = (jnp.dot(p_c.astype(BF16), vd, preferred_element_type=F32) + p_n * vnd) / denom
        for j, pair in enumerate(pairs):
            mix_ref[n, :, pair * LANES:(pair + 1) * LANES] = jnp.where(
                lane < HD_A, o[2 * j:2 * j + 1], o[2 * j + 1:2 * j + 2])
    ko_ref[n, 0:WINDOW - 1, :] = kc[1:WINDOW, :]
    ko_ref[n, WINDOW - 1:WINDOW, :] = k_new
    vo_ref[n, 0:WINDOW - 1, :] = vc[1:WINDOW, :]
    vo_ref[n, WINDOW - 1:WINDOW, :] = v_new

    for h in range(H_B):
        cols = slice(h * LANES, (h + 1) * LANES)
        lb = lb_ref[:, cols]
        f = lb + (1.0 - lb) * _sigmoid(p_ref[n, :, C_FB + h * LANES:C_FB + (h + 1) * LANES])
        a = jnp.exp(jnp.log(jnp.maximum(f, F_FLOOR)))
        q = _silu(p_ref[n, :, C_QB + h * LANES:C_QB + (h + 1) * LANES])
        v = p_ref[n, :, C_IB + h * LANES:C_IB + (h + 1) * LANES]
        s_new = _row_to_col(a) * sb_ref[n, h] + _row_to_col(1.0 - f) * v
        sbo_ref[n, h] = s_new
        o = jnp.sum(_row_to_col(q) * s_new, axis=0, keepdims=True)
        gate = p_ref[n, :, C_GB + h * LANES:C_GB + (h + 1) * LANES]
        mix_ref[n, :, W_A + h * LANES:W_A + (h + 1) * LANES] = _rms_gate(o, nwb_ref[:, cols], gate)

    la_all = _gla_log_decay(jnp.broadcast_to(p_ref[n, :, C_AC:C_AC + LANES], (SUBLANES, LANES)),
                            wa2_ref[...], ba_ref[...])[0:1]
    for pair in range(H_C // 2):
        cols = slice(pair * LANES, (pair + 1) * LANES)
        a_col = _row_to_col(jnp.exp(la_all[:, cols]))
        q_col = _row_to_col(p_ref[n, :, C_QC + pair * LANES:C_QC + (pair + 1) * LANES] * (DK_C ** -0.5))
        k_col = _row_to_col(p_ref[n, :, C_KC + pair * LANES:C_KC + (pair + 1) * LANES])
        for half in range(2):
            h = 2 * pair + half
            rows = slice(half * DK_C, (half + 1) * DK_C)
            v = p_ref[n, :, C_VC + h * DV_C:C_VC + (h + 1) * DV_C]
            s_new = a_col[rows] * sc_ref[n, h] + k_col[rows] * v
            sco_ref[n, h] = s_new
            o = jnp.sum(q_col[rows] * s_new, axis=0, keepdims=True)
            gate = p_ref[n, :, C_GC + h * DV_C:C_GC + (h + 1) * DV_C]
            hc = slice(h * DV_C, (h + 1) * DV_C)
            mix_ref[n, :, W_A + W_B + h * DV_C:W_A + W_B + (h + 1) * DV_C] = _rms_gate(o, nwc_ref[:, hc], gate)


def _sample_mixers(proj_s, cache_k, cache_v, s_b, s_c, layer, sinks, cos, sin, lb, nw_b, wa2_pad, ba, nw_c):
    nseq = proj_s.shape[0]
    wqk = H_C * DK_C

    def per_seq(shape):
        return pl.BlockSpec((SEQ_PER_STEP,) + shape, lambda b: (b,) + (0,) * len(shape))

    def per_seq_of_layer(shape):
        return pl.BlockSpec((None, SEQ_PER_STEP) + shape, lambda b: (layer, b) + (0,) * len(shape))

    def full(shape):
        return pl.BlockSpec(shape, lambda b: (0,) * len(shape))

    return pl.pallas_call(
        _sample_kernel,
        grid=(nseq // SEQ_PER_STEP,),
        in_specs=[per_seq((1, PROJ_PAD)), per_seq_of_layer((WINDOW, LANES)), per_seq_of_layer((WINDOW, LANES)),
                  per_seq_of_layer((H_B, DK_B, DV_B)), per_seq_of_layer((H_C, DK_C, DV_C)),
                  pl.BlockSpec(memory_space=pltpu.SMEM),
                  full((1, LANES)), full((1, LANES)), full((1, W_B)), full((1, W_B)),
                  full((LANES, wqk)), full((1, wqk)), full((1, W_C))],
        out_specs=[per_seq((1, D_MODEL)), per_seq((WINDOW, LANES)), per_seq((WINDOW, LANES)),
                   per_seq((H_B, DK_B, DV_B)), per_seq((H_C, DK_C, DV_C))],
        out_shape=[jax.ShapeDtypeStruct((nseq, 1, D_MODEL), F32),
                   jax.ShapeDtypeStruct((nseq, WINDOW, LANES), F32),
                   jax.ShapeDtypeStruct((nseq, WINDOW, LANES), F32),
                   jax.ShapeDtypeStruct((nseq, H_B, DK_B, DV_B), F32),
                   jax.ShapeDtypeStruct((nseq, H_C, DK_C, DV_C), F32)],
        compiler_params=_cparams(("arbitrary",)),
        name="sample_mixers",
    )(proj_s, cache_k, cache_v, s_b, s_c, sinks, cos, sin, lb.reshape(1, W_B), nw_b.reshape(1, W_B),
      wa2_pad, ba.reshape(1, wqk), nw_c.reshape(1, W_C))


N_TOP = PEER_TOPK + 1
TOP_ROWS = 24


ROUTE_UNROLL = 2
SORT_N = 16


def _top_values(x, count):
    groups = [x[r:r + SUBLANES] for r in range(0, x.shape[0], SUBLANES)]
    groups += [jnp.full((SUBLANES, x.shape[1]), NEG_INF, F32)] * (SORT_N - len(groups))
    k = 2
    while k <= SORT_N:
        j = k // 2
        while j >= 1:
            for i in range(SORT_N):
                l = i ^ j
                if l > i:
                    hi, lo = jnp.maximum(groups[i], groups[l]), jnp.minimum(groups[i], groups[l])
                    groups[i], groups[l] = (hi, lo) if (i & k) == 0 else (lo, hi)
            j //= 2
        k *= 2
    out = []
    for r in range(count):
        m = jnp.max(groups[0], axis=0, keepdims=True)
        out.append(m)
        hit = groups[0] >= m
        depth = min(SORT_N, count - r)
        for d in range(depth - 1):
            groups[d] = jnp.where(hit, groups[d + 1], groups[d])
        groups[depth - 1] = jnp.where(hit, NEG_INF, groups[depth - 1])
    return out


def _top_desc(cur, dst_ref):
    dst_ref[...] = jnp.full(dst_ref.shape, NEG_INF, F32)
    for r, m in enumerate(_top_values(cur, N_TOP)):
        dst_ref[r:r + 1, :] = m


def _route_kernel(q_ref, keys_ref, s0_ref, s1_ref, ea_ref, eb_ref, thr_ref, sc_ref, top_ref):
    tt = q_ref.shape[0]
    for p in range(2):
        qh = q_ref[:, p * LANES:(p + 1) * LANES].astype(BF16)
        sc_ref[p] = lax.dot_general(keys_ref[0, p].astype(BF16), qh, (((1,), (1,)), ((), ())),
                                    preferred_element_type=F32)

    def chunk(c, sub):
        cols = pl.ds(pl.multiple_of(c * LANES, LANES), LANES)
        v0_ref, v1_ref = top_ref.at[sub, 0], top_ref.at[sub, 1]
        s0, s1 = sc_ref[0, :, cols], sc_ref[1, :, cols]
        _top_desc(s0, v0_ref)
        _top_desc(s1, v1_ref)
        v0, v1 = v0_ref[...], v1_ref[...]
        cands = jnp.concatenate(
            [v0[0:1] + v1] + [v0[a:a + 1] + v1[0:8] for a in range(1, 8)] + [v0[8:TOP_ROWS] + v1[0:1]], axis=0)
        best = _top_values(cands, N_TOP)
        kth, below = best[PEER_TOPK - 1], best[PEER_TOPK]
        cut = jnp.where(below == NEG_INF, kth, 0.5 * (kth + below))
        top = v0[0:1] + v1[0:1]
        z = jnp.sum(jnp.where(cands >= kth, jnp.exp(cands - top), 0.0), axis=0, keepdims=True)
        s0_ref[0, :, cols] = s0
        s1_ref[0, :, cols] = s1
        ea_ref[0, :, cols] = jnp.exp(s0 - v0[0:1]) / z
        eb_ref[0, :, cols] = jnp.exp(s1 - v1[0:1])
        thr_ref[0, :, cols] = cut

    def chunks(c2, carry):
        for sub in range(ROUTE_UNROLL):
            chunk(c2 * ROUTE_UNROLL + sub, sub)
        return carry

    lax.fori_loop(0, tt // (LANES * ROUTE_UNROLL), chunks, 0)


def _peer_route(qp, keys, layer, tt=TOK_TILE):
    t = qp.shape[0]
    assert tt % (LANES * ROUTE_UNROLL) == 0
    big = pl.BlockSpec((1, N_KEYS, tt), lambda i, h: (h, 0, i))
    big_shape = jax.ShapeDtypeStruct((PEER_HEADS, N_KEYS, t), F32)
    return pl.pallas_call(
        _route_kernel,
        grid=(t // tt, PEER_HEADS),
        in_specs=[pl.BlockSpec((tt, 2 * LANES), lambda i, h: (i, h)),
                  pl.BlockSpec((None, 1, 2, N_KEYS, LANES), lambda i, h: (layer, h, 0, 0, 0))],
        out_specs=[big, big, big, big, pl.BlockSpec((1, 1, tt), lambda i, h: (h, 0, i))],
        out_shape=[big_shape, big_shape, big_shape, big_shape,
                   jax.ShapeDtypeStruct((PEER_HEADS, 1, t), F32)],
        scratch_shapes=[pltpu.VMEM((2, N_KEYS, tt), F32), pltpu.VMEM((ROUTE_UNROLL, 2, TOP_ROWS, LANES), F32)],
        compiler_params=_cparams(("arbitrary", "arbitrary")),
        name="peer_route",
    )(qp, keys)


def _gelu(x):
    return 0.5 * x * (1.0 + lax.erf(x * (2.0 ** -0.5)))


def _peer_dense_kernel(x_ref, u_ref, v_ref, s0_ref, s1_ref, ea_ref, eb_ref, cut_ref, res_ref, g_ref, b_ref,
                       y_ref, yb_ref, o_ref):
    e = pl.program_id(1)
    tt = x_ref.shape[0]

    @pl.when(e == 0)
    def _():
        o_ref[...] = jnp.zeros_like(o_ref)

    def step(row0):
        hid = lax.dot_general(u_ref[...], x_ref[...], (((1,), (1,)), ((), ())), preferred_element_type=F32)
        blocks = []
        for il in range(I_PER_TILE):
            rows = slice(il * N_KEYS, (il + 1) * N_KEYS)
            parts = []
            for c in range(tt // LANES):
                cols = slice(c * LANES, (c + 1) * LANES)
                g = jnp.zeros((N_KEYS, LANES), F32)
                for h in range(PEER_HEADS):
                    need = cut_ref[h, :, cols] - s0_ref[h, row0 + il:row0 + il + 1, cols]
                    gate = eb_ref[h, :, cols] * ea_ref[h, row0 + il:row0 + il + 1, cols]
                    g = g + jnp.where(s1_ref[h, :, cols] >= need, gate, 0.0)
                parts.append((g * _gelu(hid[rows, cols])).astype(BF16))
            blocks.append(jnp.concatenate(parts, axis=1))
        w = jnp.concatenate(blocks, axis=0)
        o_ref[...] += lax.dot_general(w, v_ref[...], (((0,), (0,)), ((), ())), preferred_element_type=F32)

    @pl.when(e % 2 == 0)
    def _():
        step(0)

    @pl.when(e % 2 == 1)
    def _():
        step(I_PER_TILE)

    @pl.when(e == pl.num_programs(1) - 1)
    def _():
        out = _res_ln(res_ref[...], o_ref[...], g_ref[...], b_ref[...])
        y_ref[...] = out
        yb_ref[...] = out.astype(BF16)


def _peer_dense_res_layernorm(xb, u, v, layer, s0, s1, ea, eb, thr, res, g, b, tt=TOK_TILE):
    t, d = xb.shape
    n_tiles = u.shape[1] // EXP_TILE
    once = pl.Buffered(1)
    big = pl.BlockSpec((PEER_HEADS, N_KEYS, tt), lambda i, e: (0, 0, i), pipeline_mode=once)
    rows = pl.BlockSpec((PEER_HEADS, 2 * I_PER_TILE, tt), lambda i, e: (0, e // 2, i))
    tok = pl.BlockSpec((tt, d), lambda i, e: (i, 0))
    vec = pl.BlockSpec((1, d), lambda i, e: (0, 0))
    return pl.pallas_call(
        _peer_dense_kernel,
        grid=(t // tt, n_tiles),
        in_specs=[pl.BlockSpec((tt, d), lambda i, e: (i, 0), pipeline_mode=once),
                  pl.BlockSpec((None, EXP_TILE, d), lambda i, e: (layer, e, 0)),
                  pl.BlockSpec((None, EXP_TILE, d), lambda i, e: (layer, e, 0)),
                  rows, big, rows, big,
                  pl.BlockSpec((PEER_HEADS, 1, tt), lambda i, e: (0, 0, i), pipeline_mode=once),
                  pl.BlockSpec((tt, d), lambda i, e: (i, 0), pipeline_mode=once), vec, vec],
        out_specs=[tok, tok],
        out_shape=[jax.ShapeDtypeStruct((t, d), F32), jax.ShapeDtypeStruct((t, d), BF16)],
        scratch_shapes=[pltpu.VMEM((tt, d), F32)],
        compiler_params=_cparams(("arbitrary", "arbitrary")),
        name="peer_dense",
    )(xb, u, v, s0, s1, ea, eb, thr, res, g.reshape(1, d), b.reshape(1, d))


def kernel(x_prompt, x_sample, cache_k, cache_v, state_hgrn, state_gla, w_in, w_out, attn_sinks,
           hgrn_norm_w, lb_logits, gla_wa2, gla_ba, gla_norm_w, ln1_g, ln1_b, ln2_g, ln2_b,
           peer_wq, peer_keys, peer_u, peer_v):
    n_batch, seq, d = x_prompt.shape
    n_dec = x_sample.shape[0]
    t_prompt = n_batch * seq
    t_real = t_prompt + n_dec
    tok_pad = -(-t_real // TOK_TILE) * TOK_TILE
    past_len = 16384

    sm = jax.nn.softmax(lb_logits.astype(F32), axis=0)
    lower = jnp.cumsum(sm, axis=0) - sm[0:1]
    cos_p, sin_p = _rope_tables(jnp.arange(seq, dtype=jnp.int32))
    cos_s, sin_s = _rope_tables(past_len + jnp.arange(1, dtype=jnp.int32))
    consts = _level_constants()
    u_bf, v_bf = peer_u.astype(BF16), peer_v.astype(BF16)
    w_in_t = jnp.swapaxes(w_in, 1, 2)
    cache_k2 = cache_k.reshape(DEPTH, n_dec, WINDOW, LANES)
    cache_v2 = cache_v.reshape(DEPTH, n_dec, WINDOW, LANES)

    x = jnp.concatenate([x_prompt.reshape(t_prompt, d), x_sample.reshape(n_dec, d),
                         jnp.zeros((tok_pad - t_real, d), F32)], axis=0)
    xb = x.astype(BF16)
    pad_rows = jnp.zeros((tok_pad - t_real, d), BF16)

    outs = [[] for _ in range(8)]
    for l in range(DEPTH):
        wa2_pad = jnp.zeros((LANES, H_C * DK_C), F32).at[:GLA_RANK].set(gla_wa2[l])
        proj = _matmul_nt(xb, w_in_t, l, PROJ_PAD, tm=768, tn=1664)
        o_a, k_keep, v_keep = _attn_prompt(proj, attn_sinks[l], cos_p, sin_p, n_batch, seq)
        o_b, o_c, s_b, s_c = _rec_prompt(proj, lower[l], hgrn_norm_w[l], wa2_pad, gla_ba[l], gla_norm_w[l],
                                         consts, n_batch, seq)
        proj_s = proj[t_prompt:t_real].reshape(n_dec, 1, PROJ_PAD)
        mix_s, k_s, v_s, sb_s, sc_s = _sample_mixers(
            proj_s, cache_k2, cache_v2, state_hgrn, state_gla, l, attn_sinks[l], cos_s, sin_s, lower[l],
            hgrn_norm_w[l], wa2_pad, gla_ba[l], gla_norm_w[l])
        mix_tail = jnp.concatenate([mix_s.reshape(n_dec, d).astype(BF16), pad_rows], axis=0)
        x1, x1b = _out_proj_res_layernorm(o_a, o_b, o_c, mix_tail, w_out, l, x, ln1_g[l], ln1_b[l])
        qp = _matmul(x1b, peer_wq, l, tm=768, tn=1024)
        s0, s1, ea, eb, thr = _peer_route(qp, peer_keys, l)
        x, xb = _peer_dense_res_layernorm(x1b, u_bf, v_bf, l, s0, s1, ea, eb, thr, x1, ln2_g[l], ln2_b[l])
        for lst, val in zip(outs, (k_keep.reshape(n_batch, WINDOW, KV_A, HD_A),
                                   v_keep.reshape(n_batch, WINDOW, KV_A, HD_A), s_b, s_c,
                                   k_s.reshape(n_dec, WINDOW, KV_A, HD_A),
                                   v_s.reshape(n_dec, WINDOW, KV_A, HD_A), sb_s, sc_s)):
            lst.append(val)

    y_prompt = x[:t_prompt].reshape(n_batch, seq, d)
    y_sample = x[t_prompt:t_real].reshape(n_dec, 1, d)
    return (y_prompt, y_sample) + tuple(jnp.stack(o) for o in outs)
```
